```python
import jax, jax.numpy as jnp
from jax import lax
import numpy as np

D_MODEL = 1024
BATCH = 16
SEQ = 2048
DEPTH = 1

GRID_W = 64
Q_BLOCK = 128
ROPE_THETA = 10000.0
EPS = 1e-6
H_A = 8
QK_NOPE = 64
QK_ROPE = 32
V_DIM_A = 64
Q_LORA = 256
KV_LORA = 128
H_B = 8
KV_B = 2
HD_B = 64
D_FF = 4 * D_MODEL
PLE_DIM = 256
IN_SPLITS = [Q_LORA, KV_LORA + QK_ROPE, H_B * HD_B, KV_B * HD_B, KV_B * HD_B, D_MODEL, D_MODEL]
D_IN = sum(IN_SPLITS)

kernel_name = "hybrid_mla_axial_gqa_gated_block"


def rmsnorm(x, g):
    xf = x.astype(jnp.float32)
    y = xf * lax.rsqrt(jnp.mean(xf * xf, axis=-1, keepdims=True) + EPS)
    return (y * g.astype(jnp.float32)).astype(x.dtype)


def rope_angles(pos, dim):
    inv = ROPE_THETA ** (-jnp.arange(0, dim, 2, dtype=jnp.float32) / dim)
    return pos.astype(jnp.float32)[:, None] * inv[None, :]


def apply_rope(x, ang):
    cos = jnp.cos(ang)[None, :, None, :].astype(x.dtype)
    sin = jnp.sin(ang)[None, :, None, :].astype(x.dtype)
    x1, x2 = jnp.split(x, 2, axis=-1)
    return jnp.concatenate([x1 * cos - x2 * sin, x1 * sin + x2 * cos], axis=-1)


def blocked_attention(q, k, v):
    B, S, Hk, G, dk = q.shape
    nb = S // Q_BLOCK
    qb = q.reshape(B, nb, Q_BLOCK, Hk, G, dk).transpose(1, 0, 2, 3, 4, 5)

    def one_block(qblk):
        s = jnp.einsum("bqhgd,bkhd->bhgqk", qblk, k).astype(jnp.float32)
        pr = jax.nn.softmax(s, axis=-1).astype(v.dtype)
        return jnp.einsum("bhgqk,bkhd->bqhgd", pr, v)

    o = lax.map(one_block, qb)
    return o.transpose(1, 0, 2, 3, 4, 5).reshape(B, S, Hk * G, v.shape[-1])


def setup_inputs(seed: int = 0) -> dict:
    key = jax.random.key(seed)
    ks = jax.random.split(key, 24)
    f32 = jnp.float32

    def w(k, shape, fan_in):
        return jax.random.normal(k, shape, f32) * (fan_in ** -0.5)

    def gain(k, shape):
        return 1.0 + 0.05 * jax.random.normal(k, shape, f32)

    L = DEPTH
    return {
        "x": jax.random.normal(ks[0], (BATCH, SEQ, D_MODEL), f32),
        "p": jax.random.normal(ks[1], (DEPTH, BATCH, SEQ, PLE_DIM), f32),
        "g_mix": gain(ks[2], (L, D_MODEL)),
        "w_in": w(ks[3], (L, D_MODEL, D_IN), D_MODEL),
        "g_qa": gain(ks[4], (L, Q_LORA)),
        "w_qb": w(ks[5], (L, Q_LORA, H_A * (QK_NOPE + QK_ROPE)), Q_LORA),
        "g_kva": gain(ks[6], (L, KV_LORA)),
        "w_kvb": w(ks[7], (L, KV_LORA, H_A * (QK_NOPE + V_DIM_A)), KV_LORA),
        "g_qn": gain(ks[8], (L, HD_B)),
        "g_kn": gain(ks[9], (L, HD_B)),
        "w_oa": w(ks[10], (L, H_A * V_DIM_A, D_MODEL), H_A * V_DIM_A),
        "w_ob": w(ks[11], (L, H_B * HD_B, D_MODEL), H_B * HD_B),
        "w_o": w(ks[12], (L, D_MODEL, D_MODEL), D_MODEL),
        "g_mlp": gain(ks[13], (L, D_MODEL)),
        "w_up": w(ks[14], (L, D_MODEL, D_FF), D_MODEL),
        "w_down": w(ks[15], (L, D_FF, D_MODEL), D_FF),
        "g_ple": gain(ks[16], (L, D_MODEL)),
        "w_ple_gate": w(ks[17], (L, D_MODEL, D_MODEL), D_MODEL),
        "w_ple": w(ks[18], (L, PLE_DIM, D_MODEL), PLE_DIM),
        "g_final": gain(ks[19], (D_MODEL,)),
    }


def reference(x, p, g_mix, w_in, g_qa, w_qb, g_kva, w_kvb, g_qn, g_kn, w_oa, w_ob, w_o,
              g_mlp, w_up, w_down, g_ple, w_ple_gate, w_ple, g_final):
    B, S, D = x.shape
    ROWS = S // GRID_W
    t = jnp.arange(S)
    row = jnp.broadcast_to(jnp.arange(ROWS)[:, None], (ROWS, GRID_W)).reshape(-1)
    col = jnp.broadcast_to(jnp.arange(GRID_W)[None, :], (ROWS, GRID_W)).reshape(-1)
    ang_1d = rope_angles(t, QK_ROPE)
    ang_row = rope_angles(row, HD_B // 2)
    ang_col = rope_angles(col, HD_B // 2)
    split_idx = list(np.cumsum(IN_SPLITS)[:-1])
    scale_a = (QK_NOPE + QK_ROPE) ** -0.5
    scale_b = HD_B ** -0.5

    for i in range(DEPTH):
        h = rmsnorm(x, g_mix[i])
        z = jnp.einsum("bsd,de->bse", h, w_in[i])
        q_lat, kv_lat, qb, kb, vb, gate_a, gate_b = jnp.split(z, split_idx, axis=-1)

        cq = rmsnorm(q_lat, g_qa[i])
        qa = jnp.einsum("bsr,re->bse", cq, w_qb[i]).reshape(B, S, H_A, QK_NOPE + QK_ROPE)
        qa_nope, qa_rope = jnp.split(qa, [QK_NOPE], axis=-1)
        qa_rope = apply_rope(qa_rope, ang_1d)
        c_kv, k_pe = jnp.split(kv_lat, [KV_LORA], axis=-1)
        c_kv = rmsnorm(c_kv, g_kva[i])
        k_pe = apply_rope(k_pe[:, :, None, :], ang_1d)
        kva = jnp.einsum("bsr,re->bse", c_kv, w_kvb[i]).reshape(B, S, H_A, QK_NOPE + V_DIM_A)
        ka_nope, va = jnp.split(kva, [QK_NOPE], axis=-1)
        qa_full = (jnp.concatenate([qa_nope, qa_rope], axis=-1) * scale_a)[:, :, :, None, :]
        ka_full = jnp.concatenate([ka_nope, jnp.broadcast_to(k_pe, (B, S, H_A, QK_ROPE))], axis=-1)
        oa = blocked_attention(qa_full, ka_full, va).reshape(B, S, H_A * V_DIM_A)
        ya = jnp.einsum("bse,ed->bsd", oa, w_oa[i])

        qb = rmsnorm(qb.reshape(B, S, H_B, HD_B), g_qn[i])
        kb = rmsnorm(kb.reshape(B, S, KV_B, HD_B), g_kn[i])
        vb = vb.reshape(B, S, KV_B, HD_B)
        qr, qc = jnp.split(qb, 2, axis=-1)
        qb = jnp.concatenate([apply_rope(qr, ang_row), apply_rope(qc, ang_col)], axis=-1)
        kr, kc = jnp.split(kb, 2, axis=-1)
        kb = jnp.concatenate([apply_rope(kr, ang_row), apply_rope(kc, ang_col)], axis=-1)
        qb = (qb * scale_b).reshape(B, S, KV_B, H_B // KV_B, HD_B)
        ob = blocked_attention(qb, kb, vb).reshape(B, S, H_B * HD_B)
        yb = jnp.einsum("bse,ed->bsd", ob, w_ob[i])

        merged = jax.nn.sigmoid(gate_a) * ya + jax.nn.sigmoid(gate_b) * yb
        x = x + jnp.einsum("bsd,de->bse", merged, w_o[i])

        h2 = rmsnorm(x, g_mlp[i])
        u = jax.nn.relu(jnp.einsum("bsd,df->bsf", h2, w_up[i]))
        x = x + jnp.einsum("bsf,fd->bsd", u * u, w_down[i])

        h3 = rmsnorm(x, g_ple[i])
        gate = jax.nn.sigmoid(jnp.einsum("bsd,de->bse", h3, w_ple_gate[i]))
        x = x + gate * jnp.einsum("bsk,kd->bsd", p[i], w_ple[i])

    return rmsnorm(x, g_final)
```

```python
import functools

import jax
import jax.numpy as jnp
from jax import lax
from jax.experimental import pallas as pl
from jax.experimental.pallas import tpu as pltpu

F32 = jnp.float32
BF16 = jnp.bfloat16

D_MODEL = 1024
GRID_W = 64
ROPE_THETA = 10000.0
EPS = 1e-6
H_A = 8
QK_NOPE = 64
QK_ROPE = 32
V_DIM_A = 64
Q_LORA = 256
KV_LORA = 128
H_B = 8
KV_B = 2
HD_B = 64
D_FF = 4 * D_MODEL
PLE_DIM = 256

LANES = 128
HALF = LANES // 2
ROT = QK_ROPE // 2

C_QLAT = 0
C_CKV = C_QLAT + Q_LORA
C_KPE = C_CKV + KV_LORA
C_QB = C_KPE + LANES
C_KB = C_QB + H_B * HD_B
C_VB = C_KB + KV_B * HD_B
C_GA = C_VB + KV_B * HD_B
C_GB = C_GA + D_MODEL
D_IN_R = C_GB + D_MODEL

TM_PROJ = 512
TQ_ATTN = 256
TM_POST = 512
FF_CHUNK = 1024
VMEM_LIMIT = 56 * 1024 * 1024


def _rms(x, g):
    ms = jnp.mean(x * x, axis=-1, keepdims=True)
    return x * lax.rsqrt(ms + EPS) * g


def _rope(x, c, s1, s2):
    return x * c + pltpu.roll(x, LANES - ROT, 1) * s1 + pltpu.roll(x, ROT, 1) * s2


def _dot(a, b):
    return jnp.dot(a, b, preferred_element_type=F32)


def _proj_kernel(x_ref, gmix_ref, win_ref, gqa_ref, wqb_ref, gkva_ref, wk_ref, wv_ref,
                 gqn_ref, gkn_ref, seg_ref, taba_ref, tabb_ref,
                 qa_ref, ka_ref, va_ref, qb_ref, kbm_ref, vbd_ref, ga_ref, gb_ref):
    h = _rms(x_ref[...], gmix_ref[...]).astype(BF16)

    cq = _rms(_dot(h, win_ref[:, C_QLAT:C_QLAT + Q_LORA]), gqa_ref[...]).astype(BF16)
    cqa, s1qa, s2qa = (taba_ref[:, i * LANES:(i + 1) * LANES] for i in range(3))
    for hd in range(H_A):
        sl = slice(hd * LANES, (hd + 1) * LANES)
        q = _dot(cq, wqb_ref[:, sl])
        qa_ref[:, sl] = _rope(q, cqa, s1qa, s2qa).astype(BF16)

    ckv = _rms(_dot(h, win_ref[:, C_CKV:C_CKV + KV_LORA]), gkva_ref[...]).astype(BF16)
    cka, s1ka, s2ka = (taba_ref[:, i * LANES:(i + 1) * LANES] for i in range(3, 6))
    kpe = _rope(_dot(h, win_ref[:, C_KPE:C_KPE + LANES]), cka, s1ka, s2ka)
    for hd in range(H_A):
        sl = slice(hd * LANES, (hd + 1) * LANES)
        ka_ref[:, sl] = (_dot(ckv, wk_ref[:, sl]) + kpe).astype(BF16)
    va_ref[...] = _dot(ckv, wv_ref[...]).astype(BF16)

    cb, s1b, s2b = (tabb_ref[:, i * LANES:(i + 1) * LANES] for i in range(3))
    zq = _dot(h, win_ref[:, C_QB:C_QB + H_B * HD_B])
    ssq = _dot((zq * zq).astype(BF16), seg_ref[...])
    yq = zq * (lax.rsqrt(ssq * (1.0 / HD_B) + EPS) * (HD_B ** -0.5)) * gqn_ref[...]
    for j in range(H_B * HD_B // LANES):
        sl = slice(j * LANES, (j + 1) * LANES)
        qb_ref[:, sl] = _rope(yq[:, sl], cb, s1b, s2b).astype(BF16)

    zk = _dot(h, win_ref[:, C_KB:C_KB + LANES])
    ssk = _dot((zk * zk).astype(BF16), seg_ref[0:LANES, 0:LANES])
    yk = zk * lax.rsqrt(ssk * (1.0 / HD_B) + EPS) * gkn_ref[...]
    kb = _rope(yk, cb, s1b, s2b)
    lane = lax.broadcasted_iota(jnp.int32, kb.shape, 1)
    lo = jnp.where(lane < HALF, kb, 0.0)
    hi = kb - lo
    kbm_ref[:, 0 * LANES:1 * LANES] = lo.astype(BF16)
    kbm_ref[:, 1 * LANES:2 * LANES] = pltpu.roll(lo, HALF, 1).astype(BF16)
    kbm_ref[:, 2 * LANES:3 * LANES] = pltpu.roll(hi, HALF, 1).astype(BF16)
    kbm_ref[:, 3 * LANES:4 * LANES] = hi.astype(BF16)

    zv = _dot(h, win_ref[:, C_VB:C_VB + LANES])
    vlo = jnp.where(lane < HALF, zv, 0.0)
    vhi = zv - vlo
    vbd_ref[:, 0:LANES] = (vlo + pltpu.roll(vlo, HALF, 1)).astype(BF16)
    vbd_ref[:, LANES:2 * LANES] = (vhi + pltpu.roll(vhi, HALF, 1)).astype(BF16)

    ga_ref[...] = jax.nn.sigmoid(_dot(h, win_ref[:, C_GA:C_GA + D_MODEL])).astype(BF16)
    gb_ref[...] = jax.nn.sigmoid(_dot(h, win_ref[:, C_GB:C_GB + D_MODEL])).astype(BF16)


def _const_spec(shape):
    return pl.BlockSpec(shape, lambda i: (0,) * len(shape), pipeline_mode=pl.Buffered(1))


def _proj_call(x2, gmix, win, gqa, wqb, gkva, wk, wv, gqn, gkn, seg, taba, tabb, seq):
    t = x2.shape[0]
    tm = TM_PROJ
    nseq = seq // tm
    row = lambda w: pl.BlockSpec((tm, w), lambda i: (i, 0))
    tab = lambda w: pl.BlockSpec((tm, w), lambda i: (i % nseq, 0))
    widths = (H_A * LANES, H_A * LANES, H_A * V_DIM_A, H_B * HD_B, 4 * LANES, 2 * LANES,
              D_MODEL, D_MODEL)
    return pl.pallas_call(
        _proj_kernel,
        grid=(t // tm,),
        in_specs=[row(D_MODEL), _const_spec(gmix.shape), _const_spec(win.shape),
                  _const_spec(gqa.shape), _const_spec(wqb.shape), _const_spec(gkva.shape),
                  _const_spec(wk.shape), _const_spec(wv.shape), _const_spec(gqn.shape),
                  _const_spec(gkn.shape), _const_spec(seg.shape),
                  tab(taba.shape[1]), tab(tabb.shape[1])],
        out_specs=[row(w) for w in widths],
        out_shape=[jax.ShapeDtypeStruct((t, w), BF16) for w in widths],
        compiler_params=pltpu.CompilerParams(dimension_semantics=("parallel",),
                                             vmem_limit_bytes=VMEM_LIMIT),
        name="proj",
    )(x2, gmix, win, gqa, wqb, gkva, wk, wv, gqn, gkn, seg, taba, tabb)


def _attn_kernel(q_ref, k_ref, v_ref, o_ref, *, heads):
    lane = lax.broadcasted_iota(jnp.int32, (q_ref.shape[1], LANES), 1)
    for j, (h0, h1, v_off) in enumerate(heads):
        v = v_ref[0, :, v_off:v_off + LANES]
        outs = []
        for q_off, k_off in (h0, h1):
            q = q_ref[0, :, q_off:q_off + LANES]
            k = k_ref[0, :, k_off:k_off + LANES]
            s = lax.dot_general(q, k, (((1,), (1,)), ((), ())), preferred_element_type=F32)
            m = jnp.max(s, axis=-1, keepdims=True)
            p = jnp.exp(s - m)
            l = jnp.sum(p, axis=-1, keepdims=True)
            outs.append(_dot(p.astype(BF16), v) / l)
        o_ref[0, :, j * LANES:(j + 1) * LANES] = jnp.where(lane < HALF, outs[0], outs[1]).astype(BF16)


def _attn_call(q, k, v, heads, name):
    b, s, _ = q.shape
    tq = TQ_ATTN
    n_out = len(heads) * LANES
    return pl.pallas_call(
        functools.partial(_attn_kernel, heads=heads),
        grid=(b, s // tq),
        in_specs=[pl.BlockSpec((1, tq, q.shape[2]), lambda i, j: (i, j, 0)),
                  pl.BlockSpec((1, s, k.shape[2]), lambda i, j: (i, 0, 0)),
                  pl.BlockSpec((1, s, v.shape[2]), lambda i, j: (i, 0, 0))],
        out_specs=pl.BlockSpec((1, tq, n_out), lambda i, j: (i, j, 0)),
        out_shape=jax.ShapeDtypeStruct((b, s, n_out), BF16),
        compiler_params=pltpu.CompilerParams(dimension_semantics=("parallel", "parallel"),
                                             vmem_limit_bytes=VMEM_LIMIT),
        name=name,
    )(q, k, v)


def _post_kernel(x_ref, oa_ref, ob_ref, ga_ref, gb_ref, p_ref,
                 woa_ref, wob_ref, wo_ref, gmlp_ref, wup_ref, wdn_ref,
                 gple_ref, wpg_ref, wple_ref, gfin_ref, out_ref):
    ya = _dot(oa_ref[...], woa_ref[...])
    yb = _dot(ob_ref[...], wob_ref[...])
    merged = ga_ref[...].astype(F32) * ya + gb_ref[...].astype(F32) * yb
    x1 = x_ref[...] + _dot(merged.astype(BF16), wo_ref[...])

    h2 = _rms(x1, gmlp_ref[...]).astype(BF16)
    acc = jnp.zeros_like(x1)
    for c in range(D_FF // FF_CHUNK):
        sl = slice(c * FF_CHUNK, (c + 1) * FF_CHUNK)
        u = jnp.maximum(_dot(h2, wup_ref[:, sl]), 0.0)
        acc = acc + _dot((u * u).astype(BF16), wdn_ref[sl, :])
    x2 = x1 + acc

    h3 = _rms(x2, gple_ref[...]).astype(BF16)
    gate = jax.nn.sigmoid(_dot(h3, wpg_ref[...]))
    x3 = x2 + gate * _dot(p_ref[...].astype(BF16), wple_ref[...])
    out_ref[...] = _rms(x3, gfin_ref[...])


def _post_call(x2, oa, ob, ga, gb, p2, woa, wob, wo, gmlp, wup, wdn, gple, wpg, wple, gfin):
    t = x2.shape[0]
    tm = TM_POST
    row = lambda w: pl.BlockSpec((tm, w), lambda i: (i, 0))
    consts = (woa, wob, wo, gmlp, wup, wdn, gple, wpg, wple, gfin)
    return pl.pallas_call(
        _post_kernel,
        grid=(t // tm,),
        in_specs=[row(D_MODEL), row(oa.shape[1]), row(ob.shape[1]), row(D_MODEL), row(D_MODEL),
                  row(PLE_DIM)] + [_const_spec(c.shape) for c in consts],
        out_specs=row(D_MODEL),
        out_shape=jax.ShapeDtypeStruct((t, D_MODEL), F32),
        compiler_params=pltpu.CompilerParams(dimension_semantics=("parallel",),
                                             vmem_limit_bytes=VMEM_LIMIT),
        name="post",
    )(x2, oa, ob, ga, gb, p2, *consts)


def _rope_tables(seq):
    t = jnp.arange(seq)
    inv = ROPE_THETA ** (-jnp.arange(0, QK_ROPE, 2, dtype=F32) / QK_ROPE)
    ang = lambda pos: pos.astype(F32)[:, None] * inv[None, :]
    a1, ar, ac = ang(t), ang(t // GRID_W), ang(t % GRID_W)
    z = lambda w: jnp.zeros((seq, w), F32)
    ca = jnp.concatenate([z(QK_NOPE), jnp.cos(a1), jnp.cos(a1), z(32)], axis=1)
    s1a = jnp.concatenate([z(QK_NOPE), -jnp.sin(a1), z(ROT), z(32)], axis=1)
    s2a = jnp.concatenate([z(QK_NOPE), z(ROT), jnp.sin(a1), z(32)], axis=1)
    nope = jnp.concatenate([jnp.ones((seq, QK_NOPE), F32), z(LANES - QK_NOPE)], axis=1)
    scale_a = (QK_NOPE + QK_ROPE) ** -0.5
    taba = jnp.concatenate([(ca + nope) * scale_a, s1a * scale_a, s2a * scale_a, ca, s1a, s2a], axis=1)
    cb = jnp.concatenate([jnp.cos(ar), jnp.cos(ar), jnp.cos(ac), jnp.cos(ac)] * 2, axis=1)
    s1b = jnp.concatenate([-jnp.sin(ar), z(ROT), -jnp.sin(ac), z(ROT)] * 2, axis=1)
    s2b = jnp.concatenate([z(ROT), jnp.sin(ar), z(ROT), jnp.sin(ac)] * 2, axis=1)
    tabb = jnp.concatenate([cb, s1b, s2b], axis=1)
    return taba, tabb


def kernel(x, p, g_mix, w_in, g_qa, w_qb, g_kva, w_kvb, g_qn, g_kn, w_oa, w_ob, w_o,
           g_mlp, w_up, w_down, g_ple, w_ple_gate, w_ple, g_final):
    b, s, d = x.shape
    t = b * s
    x2 = x.reshape(t, d)
    taba, tabb = _rope_tables(s)
    seg = (jnp.arange(H_B * HD_B)[:, None] // HD_B == jnp.arange(H_B * HD_B)[None, :] // HD_B).astype(BF16)
    row = lambda g: g.reshape(1, -1)

    assert w_in.shape[0] == 1, "the block is built for the stated depth of one layer"
    wi = w_in[0]
    o = 0
    parts = {}
    for name, w in (("qlat", Q_LORA), ("ckv", KV_LORA), ("kpe", QK_ROPE), ("qb", H_B * HD_B),
                    ("kb", KV_B * HD_B), ("vb", KV_B * HD_B), ("ga", D_MODEL), ("gb", D_MODEL)):
        parts[name] = wi[:, o:o + w]
        o += w
    zc = lambda w: jnp.zeros((d, w), wi.dtype)
    win = jnp.concatenate([parts["qlat"], parts["ckv"], zc(QK_NOPE), parts["kpe"],
                           zc(LANES - QK_NOPE - QK_ROPE), parts["qb"], parts["kb"], parts["vb"],
                           parts["ga"], parts["gb"]], axis=1).astype(BF16)
    wqb = jnp.pad(w_qb[0].reshape(Q_LORA, H_A, QK_NOPE + QK_ROPE),
                  ((0, 0), (0, 0), (0, LANES - QK_NOPE - QK_ROPE))).reshape(Q_LORA, H_A * LANES).astype(BF16)
    wkv = w_kvb[0].reshape(KV_LORA, H_A, QK_NOPE + V_DIM_A)
    wk = jnp.pad(wkv[:, :, :QK_NOPE], ((0, 0), (0, 0), (0, LANES - QK_NOPE))).reshape(KV_LORA, H_A * LANES).astype(BF16)
    wv = wkv[:, :, QK_NOPE:].reshape(KV_LORA, H_A * V_DIM_A).astype(BF16)

    qa, ka, va, qb, kbm, vbd, ga, gb = _proj_call(
        x2, row(g_mix[0]), win, row(g_qa[0]), wqb, row(g_kva[0]), wk, wv,
        row(jnp.tile(g_qn[0], H_B)), row(jnp.tile(g_kn[0], KV_B)), seg, taba, tabb, s)

    heads_a = tuple((((2 * j) * LANES, (2 * j) * LANES), ((2 * j + 1) * LANES, (2 * j + 1) * LANES), j * LANES)
                    for j in range(H_A // 2))
    oa = _attn_call(qa.reshape(b, s, -1), ka.reshape(b, s, -1), va.reshape(b, s, -1), heads_a, "attn_a")
    pairs_per_group = H_B // KV_B // 2
    heads_b = tuple(((j * LANES, (2 * (j // pairs_per_group)) * LANES),
                     (j * LANES, (2 * (j // pairs_per_group) + 1) * LANES),
                     (j // pairs_per_group) * LANES) for j in range(H_B // 2))
    ob = _attn_call(qb.reshape(b, s, -1), kbm.reshape(b, s, -1), vbd.reshape(b, s, -1), heads_b, "attn_b")

    out = _post_call(x2, oa.reshape(t, -1), ob.reshape(t, -1), ga, gb, p[0].reshape(t, -1),
                     w_oa[0].astype(BF16), w_ob[0].astype(BF16), w_o[0].astype(BF16), row(g_mlp[0]),
                     w_up[0].astype(BF16), w_down[0].astype(BF16), row(g_ple[0]),
                     w_ple_gate[0].astype(BF16), w_ple[0].astype(BF16), row(g_final))
    return out.reshape(b, s, d)
```

```python
import functools
import math

import jax
import jax.numpy as jnp
from jax import lax
from jax.experimental import pallas as pl
from jax.experimental.pallas import tpu as pltpu

F32 = jnp.float32
BF16 = jnp.bfloat16

D_MODEL = 1024
GRID_W = 64
ROPE_THETA = 10000.0
EPS = 1e-6
H_A = 8
QK_NOPE = 64
QK_ROPE = 32
V_DIM_A = 64
Q_LORA = 256
KV_LORA = 128
H_B = 8
KV_B = 2
HD_B = 64
D_FF = 4 * D_MODEL
PLE_DIM = 256

LANES = 128
HALF = LANES // 2
ROT = QK_ROPE // 2
LOG2E = math.log2(math.e)

C_QLAT = 0
C_CKV = C_QLAT + Q_LORA
C_KPE = C_CKV + KV_LORA
C_QB = C_KPE + LANES
C_KB = C_QB + H_B * HD_B
C_GA = C_KB + KV_B * HD_B
C_GB = C_GA + D_MODEL
D_IN_R = C_GB + D_MODEL

TM_PROJ = 512
TQ_ATTN = 256
KEY_CHUNK = 1024
SUBLANES = 8
TM_POST = 512
FF_CHUNK = 1024
VMEM_LIMIT = 56 * 1024 * 1024


def _rms(x, g):
    ms = jnp.mean(x * x, axis=-1, keepdims=True)
    return x * lax.rsqrt(ms + EPS) * g


def _rope(x, c, s1, s2):
    return x * c + pltpu.roll(x, LANES - ROT, 1) * s1 + pltpu.roll(x, ROT, 1) * s2


def _dot(a, b):
    return jnp.dot(a, b, preferred_element_type=F32)


def _dot_nt(a, b):
    return lax.dot_general(a, b, (((1,), (1,)), ((), ())), preferred_element_type=F32)


def _proj_kernel(x_ref, gmix_ref, win_ref, wvbt_ref, gqa_ref, wqb_ref, gkva_ref, wk_ref, wvt_ref,
                 gqn_ref, gkn_ref, seg_ref, taba_ref, tabb_ref,
                 qa_ref, ka_ref, vat_ref, qb_ref, kbm_ref, vbt_ref, ga_ref, gb_ref):
    h = _rms(x_ref[...], gmix_ref[...]).astype(BF16)

    cq = _rms(_dot(h, win_ref[:, C_QLAT:C_QLAT + Q_LORA]), gqa_ref[...]).astype(BF16)
    cqa, s1qa, s2qa = (taba_ref[:, i * LANES:(i + 1) * LANES] for i in range(3))
    for hd in range(H_A):
        sl = slice(hd * LANES, (hd + 1) * LANES)
        q = _dot(cq, wqb_ref[:, sl])
        qa_ref[:, sl] = _rope(q, cqa, s1qa, s2qa).astype(BF16)

    ckv = _rms(_dot(h, win_ref[:, C_CKV:C_CKV + KV_LORA]), gkva_ref[...]).astype(BF16)
    cka, s1ka, s2ka = (taba_ref[:, i * LANES:(i + 1) * LANES] for i in range(3, 6))
    kpe = _rope(_dot(h, win_ref[:, C_KPE:C_KPE + LANES]), cka, s1ka, s2ka)
    for hd in range(H_A):
        sl = slice(hd * LANES, (hd + 1) * LANES)
        ka_ref[:, sl] = (_dot(ckv, wk_ref[:, sl]) + kpe).astype(BF16)
    vat_ref[0] = _dot_nt(wvt_ref[...], ckv).astype(BF16)

    cb, s1b, s2b = (tabb_ref[:, i * LANES:(i + 1) * LANES] for i in range(3))
    zq = _dot(h, win_ref[:, C_QB:C_QB + H_B * HD_B])
    ssq = _dot((zq * zq).astype(BF16), seg_ref[...])
    yq = zq * (lax.rsqrt(ssq * (1.0 / HD_B) + EPS) * (HD_B ** -0.5 * LOG2E)) * gqn_ref[...]
    for j in range(H_B * HD_B // LANES):
        sl = slice(j * LANES, (j + 1) * LANES)
        qb_ref[:, sl] = _rope(yq[:, sl], cb, s1b, s2b).astype(BF16)

    zk = _dot(h, win_ref[:, C_KB:C_KB + LANES])
    ssk = _dot((zk * zk).astype(BF16), seg_ref[0:LANES, 0:LANES])
    yk = zk * lax.rsqrt(ssk * (1.0 / HD_B) + EPS) * gkn_ref[...]
    kb = _rope(yk, cb, s1b, s2b)
    lane = lax.broadcasted_iota(jnp.int32, kb.shape, 1)
    lo = jnp.where(lane < HALF, kb, 0.0)
    hi = kb - lo
    kbm_ref[:, 0 * LANES:1 * LANES] = lo.astype(BF16)
    kbm_ref[:, 1 * LANES:2 * LANES] = pltpu.roll(lo, HALF, 1).astype(BF16)
    kbm_ref[:, 2 * LANES:3 * LANES] = pltpu.roll(hi, HALF, 1).astype(BF16)
    kbm_ref[:, 3 * LANES:4 * LANES] = hi.astype(BF16)
    vbt_ref[0] = _dot_nt(wvbt_ref[...], h).astype(BF16)

    ga_ref[...] = jax.nn.sigmoid(_dot(h, win_ref[:, C_GA:C_GA + D_MODEL])).astype(BF16)
    gb_ref[...] = jax.nn.sigmoid(_dot(h, win_ref[:, C_GB:C_GB + D_MODEL])).astype(BF16)


def _const_spec(shape):
    return pl.BlockSpec(shape, lambda i: (0,) * len(shape), pipeline_mode=pl.Buffered(1))


def _proj_call(x2, gmix, win, wvbt, gqa, wqb, gkva, wk, wvt, gqn, gkn, seg, taba, tabb, batch, seq):
    t = x2.shape[0]
    tm = TM_PROJ
    nseq = seq // tm
    row = lambda w: pl.BlockSpec((tm, w), lambda i: (i, 0))
    tab = lambda w: pl.BlockSpec((tm, w), lambda i: (i % nseq, 0))
    colT = lambda r: pl.BlockSpec((1, r, tm), lambda i: (i // nseq, 0, i % nseq))
    rows2d = lambda w: jax.ShapeDtypeStruct((t, w), BF16)
    colsT = lambda r: jax.ShapeDtypeStruct((batch, r, seq), BF16)
    consts = (gmix, win, wvbt, gqa, wqb, gkva, wk, wvt, gqn, gkn, seg)
    return pl.pallas_call(
        _proj_kernel,
        grid=(t // tm,),
        in_specs=[row(D_MODEL)] + [_const_spec(c.shape) for c in consts]
                 + [tab(taba.shape[1]), tab(tabb.shape[1])],
        out_specs=[row(H_A * LANES), row(H_A * LANES), colT(H_A * V_DIM_A),
                   row(H_B * HD_B), row(4 * LANES), colT(KV_B * HD_B),
                   row(D_MODEL), row(D_MODEL)],
        out_shape=[rows2d(H_A * LANES), rows2d(H_A * LANES), colsT(H_A * V_DIM_A),
                   rows2d(H_B * HD_B), rows2d(4 * LANES), colsT(KV_B * HD_B),
                   rows2d(D_MODEL), rows2d(D_MODEL)],
        compiler_params=pltpu.CompilerParams(dimension_semantics=("parallel",),
                                             vmem_limit_bytes=VMEM_LIMIT),
        name="proj",
    )(x2, *consts, taba, tabb)


def _attn_kernel(q_ref, k_ref, vt_ref, o_ref, st_ref, *, heads, vdim):
    per_tile = LANES // vdim
    n_chunks = k_ref.shape[1] // KEY_CHUNK
    tq = q_ref.shape[1]

    def fold(x, op):
        return op(x.reshape(KEY_CHUNK // SUBLANES, SUBLANES, tq), axis=0)

    qts = {}

    def score_chunk(i, c):
        q_off, k_off, _ = heads[i]
        rows = slice(c * KEY_CHUNK, (c + 1) * KEY_CHUNK)
        if i not in qts:
            qts[i] = q_ref[0, :, q_off:q_off + LANES].T
        st = _dot(k_ref[0, rows, k_off:k_off + LANES], qts[i])
        st_ref[i % 2, rows, :] = st
        return fold(st, jnp.max)

    def combine(a, b, op):
        return b if a is None else op(a, b)

    m8 = None
    for c in range(n_chunks):
        m8 = combine(m8, score_chunk(0, c), jnp.maximum)
    outs = []
    for i, (_, _, v_off) in enumerate(heads):
        m = jnp.max(m8, axis=0, keepdims=True)
        m8, l8, acc = None, None, None
        for c in range(n_chunks):
            if i + 1 < len(heads):
                m8 = combine(m8, score_chunk(i + 1, c), jnp.maximum)
            rows = slice(c * KEY_CHUNK, (c + 1) * KEY_CHUNK)
            pt = jnp.exp2(st_ref[i % 2, rows, :] - m)
            l8 = combine(l8, fold(pt, jnp.sum), jnp.add)
            acc = combine(acc, _dot(vt_ref[0, v_off:v_off + vdim, rows], pt.astype(BF16)), jnp.add)
        outs.append(acc / jnp.sum(l8, axis=0, keepdims=True))
        if len(outs) == per_tile:
            tile = i // per_tile
            o_ref[0, :, tile * LANES:(tile + 1) * LANES] = jnp.concatenate(outs, axis=0).T.astype(BF16)
            outs = []


def _attn_call(q, k, vt, heads, vdim, name):
    b, s, _ = q.shape
    tq = TQ_ATTN
    n_out = len(heads) * vdim
    return pl.pallas_call(
        functools.partial(_attn_kernel, heads=heads, vdim=vdim),
        grid=(b, s // tq),
        in_specs=[pl.BlockSpec((1, tq, q.shape[2]), lambda i, j: (i, j, 0)),
                  pl.BlockSpec((1, s, k.shape[2]), lambda i, j: (i, 0, 0)),
                  pl.BlockSpec((1, vt.shape[1], s), lambda i, j: (i, 0, 0))],
        out_specs=pl.BlockSpec((1, tq, n_out), lambda i, j: (i, j, 0)),
        out_shape=jax.ShapeDtypeStruct((b, s, n_out), BF16),
        scratch_shapes=[pltpu.VMEM((2, s, tq), F32)],
        compiler_params=pltpu.CompilerParams(dimension_semantics=("parallel", "parallel"),
                                             vmem_limit_bytes=VMEM_LIMIT),
        name=name,
    )(q, k, vt)


def _post_kernel(x_ref, oa_ref, ob_ref, ga_ref, gb_ref, p_ref,
                 woa_ref, wob_ref, wo_ref, gmlp_ref, wup_ref, wdn_ref,
                 gple_ref, wpg_ref, wple_ref, gfin_ref, out_ref):
    ya = _dot(oa_ref[...], woa_ref[...])
    yb = _dot(ob_ref[...], wob_ref[...])
    merged = ga_ref[...].astype(F32) * ya + gb_ref[...].astype(F32) * yb
    x1 = x_ref[...] + _dot(merged.astype(BF16), wo_ref[...])

    h2 = _rms(x1, gmlp_ref[...]).astype(BF16)
    acc = jnp.zeros_like(x1)
    for c in range(D_FF // FF_CHUNK):
        sl = slice(c * FF_CHUNK, (c + 1) * FF_CHUNK)
        u = jnp.maximum(_dot(h2, wup_ref[:, sl]), 0.0)
        acc = acc + _dot((u * u).astype(BF16), wdn_ref[sl, :])
    x2 = x1 + acc

    h3 = _rms(x2, gple_ref[...]).astype(BF16)
    gate = jax.nn.sigmoid(_dot(h3, wpg_ref[...]))
    x3 = x2 + gate * _dot(p_ref[...].astype(BF16), wple_ref[...])
    out_ref[...] = _rms(x3, gfin_ref[...])


def _post_call(x2, oa, ob, ga, gb, p2, woa, wob, wo, gmlp, wup, wdn, gple, wpg, wple, gfin):
    t = x2.shape[0]
    tm = TM_POST
    row = lambda w: pl.BlockSpec((tm, w), lambda i: (i, 0))
    consts = (woa, wob, wo, gmlp, wup, wdn, gple, wpg, wple, gfin)
    return pl.pallas_call(
        _post_kernel,
        grid=(t // tm,),
        in_specs=[row(D_MODEL), row(oa.shape[1]), row(ob.shape[1]), row(D_MODEL), row(D_MODEL),
                  row(PLE_DIM)] + [_const_spec(c.shape) for c in consts],
        out_specs=row(D_MODEL),
        out_shape=jax.ShapeDtypeStruct((t, D_MODEL), F32),
        compiler_params=pltpu.CompilerParams(dimension_semantics=("parallel",),
                                             vmem_limit_bytes=VMEM_LIMIT),
        name="post",
    )(x2, oa, ob, ga, gb, p2, *consts)


def _rope_tables(seq):
    t = jnp.arange(seq)
    inv = ROPE_THETA ** (-jnp.arange(0, QK_ROPE, 2, dtype=F32) / QK_ROPE)
    ang = lambda pos: pos.astype(F32)[:, None] * inv[None, :]
    a1, ar, ac = ang(t), ang(t // GRID_W), ang(t % GRID_W)
    z = lambda w: jnp.zeros((seq, w), F32)
    ca = jnp.concatenate([z(QK_NOPE), jnp.cos(a1), jnp.cos(a1), z(32)], axis=1)
    s1a = jnp.concatenate([z(QK_NOPE), -jnp.sin(a1), z(ROT), z(32)], axis=1)
    s2a = jnp.concatenate([z(QK_NOPE), z(ROT), jnp.sin(a1), z(32)], axis=1)
    nope = jnp.concatenate([jnp.ones((seq, QK_NOPE), F32), z(LANES - QK_NOPE)], axis=1)
    scale_a = (QK_NOPE + QK_ROPE) ** -0.5 * LOG2E
    taba = jnp.concatenate([(ca + nope) * scale_a, s1a * scale_a, s2a * scale_a, ca, s1a, s2a], axis=1)
    cb = jnp.concatenate([jnp.cos(ar), jnp.cos(ar), jnp.cos(ac), jnp.cos(ac)] * 2, axis=1)
    s1b = jnp.concatenate([-jnp.sin(ar), z(ROT), -jnp.sin(ac), z(ROT)] * 2, axis=1)
    s2b = jnp.concatenate([z(ROT), jnp.sin(ar), z(ROT), jnp.sin(ac)] * 2, axis=1)
    tabb = jnp.concatenate([cb, s1b, s2b], axis=1)
    return taba, tabb


def kernel(x, p, g_mix, w_in, g_qa, w_qb, g_kva, w_kvb, g_qn, g_kn, w_oa, w_ob, w_o,
           g_mlp, w_up, w_down, g_ple, w_ple_gate, w_ple, g_final):
    b, s, d = x.shape
    t = b * s
    x2 = x.reshape(t, d)
    taba, tabb = _rope_tables(s)
    seg = (jnp.arange(H_B * HD_B)[:, None] // HD_B == jnp.arange(H_B * HD_B)[None, :] // HD_B).astype(BF16)
    row = lambda g: g.reshape(1, -1)

    assert w_in.shape[0] == 1, "the block is built for the stated depth of one layer"
    wi = w_in[0]
    o = 0
    parts = {}
    for name, w in (("qlat", Q_LORA), ("ckv", KV_LORA), ("kpe", QK_ROPE), ("qb", H_B * HD_B),
                    ("kb", KV_B * HD_B), ("vb", KV_B * HD_B), ("ga", D_MODEL), ("gb", D_MODEL)):
        parts[name] = wi[:, o:o + w]
        o += w
    zc = lambda w: jnp.zeros((d, w), wi.dtype)
    win = jnp.concatenate([parts["qlat"], parts["ckv"], zc(QK_NOPE), parts["kpe"],
                           zc(LANES - QK_NOPE - QK_ROPE), parts["qb"], parts["kb"],
                           parts["ga"], parts["gb"]], axis=1).astype(BF16)
    wvbt = parts["vb"].T.astype(BF16)
    wqb = jnp.pad(w_qb[0].reshape(Q_LORA, H_A, QK_NOPE + QK_ROPE),
                  ((0, 0), (0, 0), (0, LANES - QK_NOPE - QK_ROPE))).reshape(Q_LORA, H_A * LANES).astype(BF16)
    wkv = w_kvb[0].reshape(KV_LORA, H_A, QK_NOPE + V_DIM_A)
    wk = jnp.pad(wkv[:, :, :QK_NOPE], ((0, 0), (0, 0), (0, LANES - QK_NOPE))).reshape(KV_LORA, H_A * LANES).astype(BF16)
    wvt = wkv[:, :, QK_NOPE:].reshape(KV_LORA, H_A * V_DIM_A).T.astype(BF16)

    qa, ka, vat, qb, kbm, vbt, ga, gb = _proj_call(
        x2, row(g_mix[0]), win, wvbt, row(g_qa[0]), wqb, row(g_kva[0]), wk, wvt,
        row(jnp.tile(g_qn[0], H_B)), row(jnp.tile(g_kn[0], KV_B)), seg, taba, tabb, b, s)

    heads_a = tuple((hd * LANES, hd * LANES, hd * V_DIM_A) for hd in range(H_A))
    oa = _attn_call(qa.reshape(b, s, -1), ka.reshape(b, s, -1), vat, heads_a, V_DIM_A, "attn_a")
    group = H_B // KV_B
    heads_b = tuple(((hd // 2) * LANES, (2 * (hd // group) + hd % 2) * LANES, (hd // group) * HD_B)
                    for hd in range(H_B))
    ob = _attn_call(qb.reshape(b, s, -1), kbm.reshape(b, s, -1), vbt, heads_b, HD_B, "attn_b")

    out = _post_call(x2, oa.reshape(t, -1), ob.reshape(t, -1), ga, gb, p[0].reshape(t, -1),
                     w_oa[0].astype(BF16), w_ob[0].astype(BF16), w_o[0].astype(BF16), row(g_mlp[0]),
                     w_up[0].astype(BF16), w_down[0].astype(BF16), row(g_ple[0]),
                     w_ple_gate[0].astype(BF16), w_ple[0].astype(BF16), row(g_final))
    return out.reshape(b, s, d)
```

```python
import functools
import math

import jax
import jax.numpy as jnp
import numpy as np
from jax import lax
from jax.experimental import pallas as pl
from jax.experimental.pallas import tpu as pltpu

F32 = jnp.float32
BF16 = jnp.bfloat16

D_MODEL = 1024
GRID_W = 64
ROPE_THETA = 10000.0
EPS = 1e-6
H_A = 8
QK_NOPE = 64
QK_ROPE = 32
V_DIM_A = 64
Q_LORA = 256
KV_LORA = 128
H_B = 8
KV_B = 2
HD_B = 64
D_FF = 4 * D_MODEL
PLE_DIM = 256

LANES = 128
HALF = LANES // 2
ROT = QK_ROPE // 2
LOG2E = math.log2(math.e)

C_QLAT = 0
C_CKV = C_QLAT + Q_LORA
C_QB = C_CKV + KV_LORA + LANES
C_KV_B = C_QB + H_B * HD_B
C_GA = C_KV_B + 2 * KV_B * HD_B
C_GB = C_GA + D_MODEL
D_IN_R = C_GB + D_MODEL

TM_PROJ = 512
TQ_ATTN = 256
KEY_CHUNK = 256
SCORE_AHEAD = 2
SUBLANES = 8
BF16_ROWS = 16
V_DIM = V_DIM_A
assert HD_B == V_DIM
TM_POST = 512
FF_CHUNK = 1024
VMEM_LIMIT = 56 * 1024 * 1024


def _rms(x, g):
    ms = jnp.mean(x * x, axis=-1, keepdims=True)
    return x * lax.rsqrt(ms + EPS) * g


def _rope(x, c, s1, s2):
    return x * c + pltpu.roll(x, LANES - ROT, 1) * s1 + pltpu.roll(x, ROT, 1) * s2


def _dot(a, b):
    return jnp.dot(a, b, preferred_element_type=F32)


def _dot_nt(a, b):
    return lax.dot_general(a, b, (((1,), (1,)), ((), ())), preferred_element_type=F32)


def _proj_kernel(x_ref, gmix_ref, win_ref, gqa_ref, wqb_ref, gkva_ref, wk_ref, wvt_ref,
                 gqn_ref, gkn_ref, seg_ref, taba_ref, tabb_ref,
                 qa_ref, ka_ref, vat_ref, qb_ref, kbm_ref, vbt_ref, ga_ref, gb_ref):
    h = _rms(x_ref[...], gmix_ref[...]).astype(BF16)

    cq = _rms(_dot(h, win_ref[:, C_QLAT:C_QLAT + Q_LORA]), gqa_ref[...]).astype(BF16)
    cqa, s1qa, s2qa = (taba_ref[:, i * LANES:(i + 1) * LANES] for i in range(3))
    for hd in range(H_A):
        sl = slice(hd * LANES, (hd + 1) * LANES)
        q = _dot(cq, wqb_ref[:, sl])
        qa_ref[:, sl] = _rope(q, cqa, s1qa, s2qa).astype(BF16)

    zkv = _dot(h, win_ref[:, C_CKV:C_CKV + KV_LORA + LANES])
    ckv = _rms(zkv[:, :KV_LORA], gkva_ref[...]).astype(BF16)
    cka, s1ka, s2ka = (taba_ref[:, i * LANES:(i + 1) * LANES] for i in range(3, 6))
    kpe = _rope(zkv[:, KV_LORA:], cka, s1ka, s2ka)
    for hd in range(H_A):
        sl = slice(hd * LANES, (hd + 1) * LANES)
        ka_ref[:, sl] = (_dot(ckv, wk_ref[:, sl]) + kpe).astype(BF16)
    vat_ref[0] = _dot_nt(wvt_ref[...], ckv).astype(BF16)

    cb, s1b, s2b = (tabb_ref[:, i * LANES:(i + 1) * LANES] for i in range(3))
    zq = _dot(h, win_ref[:, C_QB:C_QB + H_B * HD_B])
    ssq = _dot((zq * zq).astype(BF16), seg_ref[...])
    yq = zq * (lax.rsqrt(ssq * (1.0 / HD_B) + EPS) * (HD_B ** -0.5 * LOG2E)) * gqn_ref[...]
    for j in range(H_B * HD_B // LANES):
        sl = slice(j * LANES, (j + 1) * LANES)
        qb_ref[:, sl] = _rope(yq[:, sl], cb, s1b, s2b).astype(BF16)

    zkvb = _dot(h, win_ref[:, C_KV_B:C_KV_B + 2 * LANES])
    zk = zkvb[:, :LANES]
    ssk = _dot((zk * zk).astype(BF16), seg_ref[0:LANES, 0:LANES])
    yk = zk * lax.rsqrt(ssk * (1.0 / HD_B) + EPS) * gkn_ref[...]
    kb = _rope(yk, cb, s1b, s2b)
    lane = lax.broadcasted_iota(jnp.int32, kb.shape, 1)
    lo = jnp.where(lane < HALF, kb, 0.0)
    hi = kb - lo
    kbm_ref[:, 0 * LANES:1 * LANES] = lo.astype(BF16)
    kbm_ref[:, 1 * LANES:2 * LANES] = pltpu.roll(lo, HALF, 1).astype(BF16)
    kbm_ref[:, 2 * LANES:3 * LANES] = pltpu.roll(hi, HALF, 1).astype(BF16)
    kbm_ref[:, 3 * LANES:4 * LANES] = hi.astype(BF16)
    vbt_ref[0] = zkvb[:, LANES:].T.astype(BF16)

    ga_ref[...] = _dot(h, win_ref[:, C_GA:C_GA + D_MODEL]).astype(BF16)
    gb_ref[...] = _dot(h, win_ref[:, C_GB:C_GB + D_MODEL]).astype(BF16)


def _const_spec(shape):
    return pl.BlockSpec(shape, lambda i: (0,) * len(shape), pipeline_mode=pl.Buffered(1))


def _proj_call(x2, gmix, win, gqa, wqb, gkva, wk, wvt, gqn, gkn, seg, taba, tabb, batch, seq):
    t = x2.shape[0]
    tm = TM_PROJ
    nseq = seq // tm
    row = lambda w: pl.BlockSpec((tm, w), lambda i: (i, 0))
    tab = lambda w: pl.BlockSpec((tm, w), lambda i: (i % nseq, 0))
    colT = lambda r: pl.BlockSpec((1, r, tm), lambda i: (i // nseq, 0, i % nseq))
    rows2d = lambda w: jax.ShapeDtypeStruct((t, w), BF16)
    colsT = lambda r: jax.ShapeDtypeStruct((batch, r, seq), BF16)
    consts = (gmix, win, gqa, wqb, gkva, wk, wvt, gqn, gkn, seg)
    return pl.pallas_call(
        _proj_kernel,
        grid=(t // tm,),
        in_specs=[row(D_MODEL)] + [_const_spec(c.shape) for c in consts]
                 + [tab(taba.shape[1]), tab(tabb.shape[1])],
        out_specs=[row(H_A * LANES), row(H_A * LANES), colT(H_A * V_DIM_A),
                   row(H_B * HD_B), row(4 * LANES), colT(KV_B * HD_B),
                   row(D_MODEL), row(D_MODEL)],
        out_shape=[rows2d(H_A * LANES), rows2d(H_A * LANES), colsT(H_A * V_DIM_A),
                   rows2d(H_B * HD_B), rows2d(4 * LANES), colsT(KV_B * HD_B),
                   rows2d(D_MODEL), rows2d(D_MODEL)],
        compiler_params=pltpu.CompilerParams(dimension_semantics=("parallel",),
                                             vmem_limit_bytes=VMEM_LIMIT),
        name="proj",
    )(x2, *consts, taba, tabb)


def _attn_kernel(qa_ref, ka_ref, vat_ref, qb_ref, kb_ref, vbt_ref, oa_ref, ob_ref,
                 st0_ref, st1_ref, st2_ref, pt_ref, *, heads):
    refs = ((qa_ref, ka_ref, vat_ref, oa_ref), (qb_ref, kb_ref, vbt_ref, ob_ref))
    st_refs = (st0_ref, st1_ref, st2_ref)
    per_tile = LANES // V_DIM
    n_chunks = ka_ref.shape[1] // KEY_CHUNK
    tq = qa_ref.shape[1]

    def combine(a, b, op):
        return b if a is None else op(a, b)

    def scores(i, c):
        mixer, q_off, k_off = heads[i][:3]
        q_ref, k_ref = refs[mixer][:2]
        rows = slice(c * KEY_CHUNK, (c + 1) * KEY_CHUNK)
        st = _dot_nt(k_ref[0, rows, k_off:k_off + LANES], q_ref[0, :, q_off:q_off + LANES])
        st_refs[i % len(st_refs)][rows, :] = st
        return jnp.max(st.reshape(KEY_CHUNK // SUBLANES, SUBLANES, tq), axis=0)

    acc = [None] * len(heads)
    l8 = [None] * len(heads)
    outs = {}

    def pv(i, c):
        mixer, _, _, v_off, tile = heads[i]
        vt_ref, o_ref = refs[mixer][2:]
        rows = slice(c * KEY_CHUNK, (c + 1) * KEY_CHUNK)
        acc[i] = combine(acc[i], _dot(vt_ref[0, v_off:v_off + V_DIM, rows], pt_ref[(i * n_chunks + c) % 2]),
                         jnp.add)
        if c == n_chunks - 1:
            done = outs.setdefault((mixer, tile), [])
            done.append(acc[i] / jnp.sum(l8[i], axis=0, keepdims=True))
            acc[i] = None
            if len(done) == per_tile:
                o_ref[0, :, tile * LANES:(tile + 1) * LANES] = jnp.concatenate(done, axis=0).T.astype(BF16)

    m8 = [None] * len(heads)

    def scores_into(i, c):
        m8[i] = combine(m8[i], scores(i, c), jnp.maximum)

    for i in range(min(SCORE_AHEAD, len(heads))):
        for c in range(n_chunks):
            scores_into(i, c)
    prev = None
    for i in range(len(heads)):
        m = jnp.max(m8[i], axis=0, keepdims=True)
        for c in range(n_chunks):
            if i + SCORE_AHEAD < len(heads):
                scores_into(i + SCORE_AHEAD, c)
            if prev is not None:
                pv(*prev)
            rows = slice(c * KEY_CHUNK, (c + 1) * KEY_CHUNK)
            pt = jnp.exp2(st_refs[i % len(st_refs)][rows, :] - m)
            l8[i] = combine(l8[i], jnp.sum(pt.reshape(KEY_CHUNK // SUBLANES, SUBLANES, tq), axis=0), jnp.add)
            pt_ref[(i * n_chunks + c) % 2] = pt.astype(BF16)
            prev = (i, c)
    pv(*prev)


def _attn_call(qa, ka, vat, qb, kb, vbt, heads):
    b, s, _ = qa.shape
    tq = TQ_ATTN
    qspec = lambda a: pl.BlockSpec((1, tq, a.shape[2]), lambda i, j: (i, j, 0))
    kspec = lambda a: pl.BlockSpec((1, s, a.shape[2]), lambda i, j: (i, 0, 0))
    vspec = lambda a: pl.BlockSpec((1, a.shape[1], s), lambda i, j: (i, 0, 0))
    n_out = [sum(1 for h in heads if h[0] == mixer) * V_DIM for mixer in (0, 1)]
    return pl.pallas_call(
        functools.partial(_attn_kernel, heads=heads),
        grid=(b, s // tq),
        in_specs=[qspec(qa), kspec(ka), vspec(vat), qspec(qb), kspec(kb), vspec(vbt)],
        out_specs=[pl.BlockSpec((1, tq, n), lambda i, j: (i, j, 0)) for n in n_out],
        out_shape=[jax.ShapeDtypeStruct((b, s, n), BF16) for n in n_out],
        scratch_shapes=[pltpu.VMEM((s, tq), F32)] * (SCORE_AHEAD + 1) + [
                        pltpu.VMEM((2, KEY_CHUNK, tq), BF16)],
        compiler_params=pltpu.CompilerParams(dimension_semantics=("parallel", "parallel"),
                                             vmem_limit_bytes=VMEM_LIMIT),
        name="attn",
    )(qa, ka, vat, qb, kb, vbt)


def _post_kernel(x_ref, oa_ref, ob_ref, ga_ref, gb_ref, p_ref,
                 woa_ref, wob_ref, wo_ref, gmlp_ref, wup_ref, wdn_ref,
                 gple_ref, wpg_ref, wple_ref, gfin_ref, out_ref):
    ya = _dot(oa_ref[...], woa_ref[...])
    yb = _dot(ob_ref[...], wob_ref[...])
    merged = jax.nn.sigmoid(ga_ref[...].astype(F32)) * ya + jax.nn.sigmoid(gb_ref[...].astype(F32)) * yb
    x1 = x_ref[...] + _dot(merged.astype(BF16), wo_ref[...])

    h2 = _rms(x1, gmlp_ref[...]).astype(BF16)
    acc = jnp.zeros_like(x1)
    for c in range(D_FF // FF_CHUNK):
        sl = slice(c * FF_CHUNK, (c + 1) * FF_CHUNK)
        u = jnp.maximum(_dot(h2, wup_ref[:, sl]), 0.0)
        acc = acc + _dot((u * u).astype(BF16), wdn_ref[sl, :])
    x2 = x1 + acc

    h3 = _rms(x2, gple_ref[...]).astype(BF16)
    gate = jax.nn.sigmoid(_dot(h3, wpg_ref[...]))
    x3 = x2 + gate * _dot(p_ref[...].astype(BF16), wple_ref[...])
    out_ref[...] = _rms(x3, gfin_ref[...])


def _post_call(x2, oa, ob, ga, gb, p2, woa, wob, wo, gmlp, wup, wdn, gple, wpg, wple, gfin):
    t = x2.shape[0]
    tm = TM_POST
    row = lambda w: pl.BlockSpec((tm, w), lambda i: (i, 0))
    consts = (woa, wob, wo, gmlp, wup, wdn, gple, wpg, wple, gfin)
    return pl.pallas_call(
        _post_kernel,
        grid=(t // tm,),
        in_specs=[row(D_MODEL), row(oa.shape[1]), row(ob.shape[1]), row(D_MODEL), row(D_MODEL),
                  row(PLE_DIM)] + [_const_spec(c.shape) for c in consts],
        out_specs=row(D_MODEL),
        out_shape=jax.ShapeDtypeStruct((t, D_MODEL), F32),
        compiler_params=pltpu.CompilerParams(dimension_semantics=("parallel",),
                                             vmem_limit_bytes=VMEM_LIMIT),
        name="post",
    )(x2, oa, ob, ga, gb, p2, *consts)


def _rope_tables(seq):
    t = np.arange(seq)
    inv = ROPE_THETA ** (-np.arange(0, QK_ROPE, 2, dtype=np.float64) / QK_ROPE)
    ang = lambda pos: pos.astype(np.float64)[:, None] * inv[None, :]
    a1, ar, ac = ang(t), ang(t // GRID_W), ang(t % GRID_W)
    z = lambda w: np.zeros((seq, w), np.float64)
    cat = lambda parts: np.concatenate(parts, axis=1)
    ca = cat([z(QK_NOPE), np.cos(a1), np.cos(a1), z(32)])
    s1a = cat([z(QK_NOPE), -np.sin(a1), z(ROT), z(32)])
    s2a = cat([z(QK_NOPE), z(ROT), np.sin(a1), z(32)])
    nope = cat([np.ones((seq, QK_NOPE)), z(LANES - QK_NOPE)])
    scale_a = (QK_NOPE + QK_ROPE) ** -0.5 * LOG2E
    taba = cat([(ca + nope) * scale_a, s1a * scale_a, s2a * scale_a, ca, s1a, s2a])
    cb = cat([np.cos(ar), np.cos(ar), np.cos(ac), np.cos(ac)] * 2)
    s1b = cat([-np.sin(ar), z(ROT), -np.sin(ac), z(ROT)] * 2)
    s2b = cat([z(ROT), np.sin(ar), z(ROT), np.sin(ac)] * 2)
    tabb = cat([cb, s1b, s2b])
    return jnp.asarray(taba, F32), jnp.asarray(tabb, F32)


def kernel(x, p, g_mix, w_in, g_qa, w_qb, g_kva, w_kvb, g_qn, g_kn, w_oa, w_ob, w_o,
           g_mlp, w_up, w_down, g_ple, w_ple_gate, w_ple, g_final):
    b, s, d = x.shape
    t = b * s
    x2 = x.reshape(t, d)
    taba, tabb = _rope_tables(s)
    seg = (jnp.arange(H_B * HD_B)[:, None] // HD_B == jnp.arange(H_B * HD_B)[None, :] // HD_B).astype(BF16)
    row = lambda g: g.reshape(1, -1)

    assert w_in.shape[0] == 1, "the block is built for the stated depth of one layer"
    wi = w_in[0]
    o = 0
    parts = {}
    for name, w in (("qlat", Q_LORA), ("ckv", KV_LORA), ("kpe", QK_ROPE), ("qb", H_B * HD_B),
                    ("kb", KV_B * HD_B), ("vb", KV_B * HD_B), ("ga", D_MODEL), ("gb", D_MODEL)):
        parts[name] = wi[:, o:o + w]
        o += w
    zc = lambda w: jnp.zeros((d, w), wi.dtype)
    win = jnp.concatenate([parts["qlat"], parts["ckv"], zc(QK_NOPE), parts["kpe"],
                           zc(LANES - QK_NOPE - QK_ROPE), parts["qb"], parts["kb"], parts["vb"],
                           parts["ga"], parts["gb"]], axis=1).astype(BF16)
    wqb = jnp.pad(w_qb[0].reshape(Q_LORA, H_A, QK_NOPE + QK_ROPE),
                  ((0, 0), (0, 0), (0, LANES - QK_NOPE - QK_ROPE))).reshape(Q_LORA, H_A * LANES).astype(BF16)
    wkv = w_kvb[0].reshape(KV_LORA, H_A, QK_NOPE + V_DIM_A)
    wk = jnp.pad(wkv[:, :, :QK_NOPE], ((0, 0), (0, 0), (0, LANES - QK_NOPE))).reshape(KV_LORA, H_A * LANES).astype(BF16)
    wvt = wkv[:, :, QK_NOPE:].reshape(KV_LORA, H_A * V_DIM_A).T.astype(BF16)

    qa, ka, vat, qb, kbm, vbt, ga, gb = _proj_call(
        x2, row(g_mix[0]), win, row(g_qa[0]), wqb, row(g_kva[0]), wk, wvt,
        row(jnp.tile(g_qn[0], H_B)), row(jnp.tile(g_kn[0], KV_B)), seg, taba, tabb, b, s)

    group = H_B // KV_B
    per_tile = LANES // V_DIM
    heads = tuple((0, hd * LANES, hd * LANES, hd * V_DIM, hd // per_tile) for hd in range(H_A))
    heads += tuple((1, (hd // 2) * LANES, (2 * (hd // group) + hd % 2) * LANES, (hd // group) * V_DIM,
                    hd // per_tile) for hd in range(H_B))
    oa, ob = _attn_call(qa.reshape(b, s, -1), ka.reshape(b, s, -1), vat,
                        qb.reshape(b, s, -1), kbm.reshape(b, s, -1), vbt, heads)

    out = _post_call(x2, oa.reshape(t, -1), ob.reshape(t, -1), ga, gb, p[0].reshape(t, -1),
                     w_oa[0].astype(BF16), w_ob[0].astype(BF16), w_o[0].astype(BF16), row(g_mlp[0]),
                     w_up[0].astype(BF16), w_down[0].astype(BF16), row(g_ple[0]),
                     w_ple_gate[0].astype(BF16), w_ple[0].astype(BF16), row(g_final))
    return out.reshape(b, s, d)
```

```python
import functools
import math

import jax
import jax.numpy as jnp
import numpy as np
from jax import lax
from jax.experimental import pallas as pl
from jax.experimental.pallas import tpu as pltpu

F32 = jnp.float32
BF16 = jnp.bfloat16

D_MODEL = 1024
GRID_W = 64
ROPE_THETA = 10000.0
EPS = 1e-6
H_A = 8
QK_NOPE = 64
QK_ROPE = 32
V_DIM_A = 64
Q_LORA = 256
KV_LORA = 128
H_B = 8
KV_B = 2
HD_B = 64
D_FF = 4 * D_MODEL
PLE_DIM = 256

LANES = 128
HALF = LANES // 2
ROT = QK_ROPE // 2
LOG2E = math.log2(math.e)

C_QLAT = 0
C_CKV = C_QLAT + Q_LORA
C_QB = C_CKV + KV_LORA + LANES
C_KV_B = C_QB + H_B * HD_B
C_GA = C_KV_B + 2 * KV_B * HD_B
C_GB = C_GA + D_MODEL
D_IN_R = C_GB + D_MODEL

TM_PROJ = 512
TQ_ATTN = 256
Q_TILES_PER_STEP = 2
KEY_CHUNK = 256
SCORE_AHEAD = 2
SUBLANES = 8
V_DIM = V_DIM_A
assert HD_B == V_DIM
TM_POST = 512
FF_CHUNK = 1024
VMEM_LIMIT = 56 * 1024 * 1024


def _rms(x, g):
    ms = jnp.mean(x * x, axis=-1, keepdims=True)
    return x * lax.rsqrt(ms + EPS) * g


def _rope(x, c, s1, s2):
    return x * c + pltpu.roll(x, LANES - ROT, 1) * s1 + pltpu.roll(x, ROT, 1) * s2


def _dot(a, b):
    return jnp.dot(a, b, preferred_element_type=F32)


def _dot_nt(a, b):
    return lax.dot_general(a, b, (((1,), (1,)), ((), ())), preferred_element_type=F32)


def _proj_kernel(x_ref, gmix_ref, win_ref, gqa_ref, wqb_ref, gkva_ref, wk_ref, wvt_ref,
                 gqn_ref, gkn_ref, seg_ref, taba_ref, tabb_ref,
                 qa_ref, ka_ref, vat_ref, qb_ref, kbm_ref, vbt_ref, ga_ref, gb_ref):
    h = _rms(x_ref[...], gmix_ref[...]).astype(BF16)

    cq = _rms(_dot(h, win_ref[:, C_QLAT:C_QLAT + Q_LORA]), gqa_ref[...]).astype(BF16)
    cqa, s1qa, s2qa = (taba_ref[:, i * LANES:(i + 1) * LANES] for i in range(3))
    q = _dot(cq, wqb_ref[...])
    for hd in range(H_A):
        sl = slice(hd * LANES, (hd + 1) * LANES)
        qa_ref[:, sl] = _rope(q[:, sl], cqa, s1qa, s2qa).astype(BF16)

    zkv = _dot(h, win_ref[:, C_CKV:C_CKV + KV_LORA + LANES])
    ckv = _rms(zkv[:, :KV_LORA], gkva_ref[...]).astype(BF16)
    cka, s1ka, s2ka = (taba_ref[:, i * LANES:(i + 1) * LANES] for i in range(3, 6))
    kpe = _rope(zkv[:, KV_LORA:], cka, s1ka, s2ka)
    kn = _dot(ckv, wk_ref[...])
    for hd in range(H_A):
        sl = slice(hd * LANES, (hd + 1) * LANES)
        ka_ref[:, sl] = (kn[:, sl] + kpe).astype(BF16)
    vat_ref[0] = _dot_nt(wvt_ref[...], ckv).astype(BF16)

    cb, s1b, s2b = (tabb_ref[:, i * LANES:(i + 1) * LANES] for i in range(3))
    zq = _dot(h, win_ref[:, C_QB:C_QB + H_B * HD_B])
    ssq = _dot((zq * zq).astype(BF16), seg_ref[...])
    yq = zq * (lax.rsqrt(ssq * (1.0 / HD_B) + EPS) * (HD_B ** -0.5 * LOG2E)) * gqn_ref[...]
    for j in range(H_B * HD_B // LANES):
        sl = slice(j * LANES, (j + 1) * LANES)
        qb_ref[:, sl] = _rope(yq[:, sl], cb, s1b, s2b).astype(BF16)

    zkvb = _dot(h, win_ref[:, C_KV_B:C_KV_B + 2 * LANES])
    zk = zkvb[:, :LANES]
    ssk = _dot((zk * zk).astype(BF16), seg_ref[0:LANES, 0:LANES])
    yk = zk * lax.rsqrt(ssk * (1.0 / HD_B) + EPS) * gkn_ref[...]
    kb = _rope(yk, cb, s1b, s2b)
    lane = lax.broadcasted_iota(jnp.int32, kb.shape, 1)
    lo = jnp.where(lane < HALF, kb, 0.0)
    hi = kb - lo
    kbm_ref[:, 0 * LANES:1 * LANES] = lo.astype(BF16)
    kbm_ref[:, 1 * LANES:2 * LANES] = pltpu.roll(lo, HALF, 1).astype(BF16)
    kbm_ref[:, 2 * LANES:3 * LANES] = pltpu.roll(hi, HALF, 1).astype(BF16)
    kbm_ref[:, 3 * LANES:4 * LANES] = hi.astype(BF16)
    vbt_ref[0] = zkvb[:, LANES:].T.astype(BF16)

    ga_ref[...] = _dot(h, win_ref[:, C_GA:C_GA + D_MODEL]).astype(BF16)
    gb_ref[...] = _dot(h, win_ref[:, C_GB:C_GB + D_MODEL]).astype(BF16)


def _const_spec(shape):
    return pl.BlockSpec(shape, lambda i: (0,) * len(shape), pipeline_mode=pl.Buffered(1))


def _proj_call(x2, gmix, win, gqa, wqb, gkva, wk, wvt, gqn, gkn, seg, taba, tabb, batch, seq):
    t = x2.shape[0]
    tm = TM_PROJ
    nseq = seq // tm
    row = lambda w: pl.BlockSpec((tm, w), lambda i: (i, 0))
    tab = lambda w: pl.BlockSpec((tm, w), lambda i: (i % nseq, 0))
    colT = lambda r: pl.BlockSpec((1, r, tm), lambda i: (i // nseq, 0, i % nseq))
    rows2d = lambda w: jax.ShapeDtypeStruct((t, w), BF16)
    colsT = lambda r: jax.ShapeDtypeStruct((batch, r, seq), BF16)
    consts = (gmix, win, gqa, wqb, gkva, wk, wvt, gqn, gkn, seg)
    return pl.pallas_call(
        _proj_kernel,
        grid=(t // tm,),
        in_specs=[row(D_MODEL)] + [_const_spec(c.shape) for c in consts]
                 + [tab(taba.shape[1]), tab(tabb.shape[1])],
        out_specs=[row(H_A * LANES), row(H_A * LANES), colT(H_A * V_DIM_A),
                   row(H_B * HD_B), row(4 * LANES), colT(KV_B * HD_B),
                   row(D_MODEL), row(D_MODEL)],
        out_shape=[rows2d(H_A * LANES), rows2d(H_A * LANES), colsT(H_A * V_DIM_A),
                   rows2d(H_B * HD_B), rows2d(4 * LANES), colsT(KV_B * HD_B),
                   rows2d(D_MODEL), rows2d(D_MODEL)],
        compiler_params=pltpu.CompilerParams(dimension_semantics=("parallel",),
                                             vmem_limit_bytes=VMEM_LIMIT),
        name="proj",
    )(x2, *consts, taba, tabb)


def _attn_kernel(qa_ref, ka_ref, vat_ref, qb_ref, kb_ref, vbt_ref, oa_ref, ob_ref,
                 st0_ref, st1_ref, st2_ref, pt_ref, *, heads):
    refs = ((qa_ref, ka_ref, vat_ref, oa_ref), (qb_ref, kb_ref, vbt_ref, ob_ref))
    st_refs = (st0_ref, st1_ref, st2_ref)
    per_tile = LANES // V_DIM
    n_chunks = ka_ref.shape[1] // KEY_CHUNK
    tq = TQ_ATTN

    def combine(a, b, op):
        return b if a is None else op(a, b)

    def scores(i, c):
        mixer, q_off, k_off = heads[i][:3]
        q_ref, k_ref = refs[mixer][:2]
        rows = slice(c * KEY_CHUNK, (c + 1) * KEY_CHUNK)
        q_rows = slice(heads[i][5] * tq, (heads[i][5] + 1) * tq)
        st = _dot_nt(k_ref[0, rows, k_off:k_off + LANES], q_ref[0, q_rows, q_off:q_off + LANES])
        st_refs[i % len(st_refs)][rows, :] = st
        return jnp.max(st.reshape(KEY_CHUNK // SUBLANES, SUBLANES, tq), axis=0)

    acc = [None] * len(heads)
    l8 = [None] * len(heads)
    outs = {}

    def pv(i, c):
        mixer, _, _, v_off, tile, q_tile = heads[i]
        vt_ref, o_ref = refs[mixer][2:]
        rows = slice(c * KEY_CHUNK, (c + 1) * KEY_CHUNK)
        acc[i] = combine(acc[i], _dot(vt_ref[0, v_off:v_off + V_DIM, rows], pt_ref[(i * n_chunks + c) % 2]),
                         jnp.add)
        if c == n_chunks - 1:
            done = outs.setdefault((mixer, tile, q_tile), [])
            done.append(acc[i] / jnp.sum(l8[i], axis=0, keepdims=True))
            acc[i] = None
            if len(done) == per_tile:
                o_ref[0, q_tile * tq:(q_tile + 1) * tq, tile * LANES:(tile + 1) * LANES] = (
                    jnp.concatenate(done, axis=0).T.astype(BF16))

    m8 = [None] * len(heads)

    def scores_into(i, c):
        m8[i] = combine(m8[i], scores(i, c), jnp.maximum)

    for i in range(min(SCORE_AHEAD, len(heads))):
        for c in range(n_chunks):
            scores_into(i, c)
    prev = None
    for i in range(len(heads)):
        m = jnp.max(m8[i], axis=0, keepdims=True)
        for c in range(n_chunks):
            if i + SCORE_AHEAD < len(heads):
                scores_into(i + SCORE_AHEAD, c)
            if prev is not None:
                pv(*prev)
            rows = slice(c * KEY_CHUNK, (c + 1) * KEY_CHUNK)
            pt = jnp.exp2(st_refs[i % len(st_refs)][rows, :] - m)
            l8[i] = combine(l8[i], jnp.sum(pt.reshape(KEY_CHUNK // SUBLANES, SUBLANES, tq), axis=0), jnp.add)
            pt_ref[(i * n_chunks + c) % 2] = pt.astype(BF16)
            prev = (i, c)
    pv(*prev)


def _attn_call(qa, ka, vat, qb, kb, vbt, heads):
    b, s, _ = qa.shape
    tq = TQ_ATTN * Q_TILES_PER_STEP
    heads = tuple(h + (qt,) for qt in range(Q_TILES_PER_STEP) for h in heads)
    qspec = lambda a: pl.BlockSpec((1, tq, a.shape[2]), lambda i, j: (i, j, 0))
    kspec = lambda a: pl.BlockSpec((1, s, a.shape[2]), lambda i, j: (i, 0, 0))
    vspec = lambda a: pl.BlockSpec((1, a.shape[1], s), lambda i, j: (i, 0, 0))
    n_out = [sum(1 for h in heads if h[0] == mixer) * V_DIM // Q_TILES_PER_STEP for mixer in (0, 1)]
    return pl.pallas_call(
        functools.partial(_attn_kernel, heads=heads),
        grid=(b, s // tq),
        in_specs=[qspec(qa), kspec(ka), vspec(vat), qspec(qb), kspec(kb), vspec(vbt)],
        out_specs=[pl.BlockSpec((1, tq, n), lambda i, j: (i, j, 0)) for n in n_out],
        out_shape=[jax.ShapeDtypeStruct((b, s, n), BF16) for n in n_out],
        scratch_shapes=[pltpu.VMEM((s, TQ_ATTN), F32)] * (SCORE_AHEAD + 1) + [
                        pltpu.VMEM((2, KEY_CHUNK, TQ_ATTN), BF16)],
        compiler_params=pltpu.CompilerParams(dimension_semantics=("parallel", "parallel"),
                                             vmem_limit_bytes=VMEM_LIMIT),
        name="attn",
    )(qa, ka, vat, qb, kb, vbt)


def _post_kernel(x_ref, oa_ref, ob_ref, ga_ref, gb_ref, p_ref,
                 woa_ref, wob_ref, wo_ref, gmlp_ref, wup_ref, wdn_ref,
                 gple_ref, wpg_ref, wple_ref, gfin_ref, out_ref):
    ya = _dot(oa_ref[...], woa_ref[...])
    yb = _dot(ob_ref[...], wob_ref[...])
    merged = jax.nn.sigmoid(ga_ref[...].astype(F32)) * ya + jax.nn.sigmoid(gb_ref[...].astype(F32)) * yb
    x1 = x_ref[...] + _dot(merged.astype(BF16), wo_ref[...])

    h2 = _rms(x1, gmlp_ref[...]).astype(BF16)
    acc = jnp.zeros_like(x1)
    for c in range(D_FF // FF_CHUNK):
        sl = slice(c * FF_CHUNK, (c + 1) * FF_CHUNK)
        u = jnp.maximum(_dot(h2, wup_ref[:, sl]), 0.0)
        acc = acc + _dot((u * u).astype(BF16), wdn_ref[sl, :])
    x2 = x1 + acc

    h3 = _rms(x2, gple_ref[...]).astype(BF16)
    gate = jax.nn.sigmoid(_dot(h3, wpg_ref[...]))
    x3 = x2 + gate * _dot(p_ref[...].astype(BF16), wple_ref[...])
    out_ref[...] = _rms(x3, gfin_ref[...])


def _post_call(x2, oa, ob, ga, gb, p2, woa, wob, wo, gmlp, wup, wdn, gple, wpg, wple, gfin):
    t = x2.shape[0]
    tm = TM_POST
    row = lambda w: pl.BlockSpec((tm, w), lambda i: (i, 0))
    consts = (woa, wob, wo, gmlp, wup, wdn, gple, wpg, wple, gfin)
    return pl.pallas_call(
        _post_kernel,
        grid=(t // tm,),
        in_specs=[row(D_MODEL), row(oa.shape[1]), row(ob.shape[1]), row(D_MODEL), row(D_MODEL),
                  row(PLE_DIM)] + [_const_spec(c.shape) for c in consts],
        out_specs=row(D_MODEL),
        out_shape=jax.ShapeDtypeStruct((t, D_MODEL), F32),
        compiler_params=pltpu.CompilerParams(dimension_semantics=("parallel",),
                                             vmem_limit_bytes=VMEM_LIMIT),
        name="post",
    )(x2, oa, ob, ga, gb, p2, *consts)


def _rope_tables(seq):
    t = np.arange(seq)
    inv = ROPE_THETA ** (-np.arange(0, QK_ROPE, 2, dtype=np.float64) / QK_ROPE)
    ang = lambda pos: pos.astype(np.float64)[:, None] * inv[None, :]
    a1, ar, ac = ang(t), ang(t // GRID_W), ang(t % GRID_W)
    z = lambda w: np.zeros((seq, w), np.float64)
    cat = lambda parts: np.concatenate(parts, axis=1)
    ca = cat([z(QK_NOPE), np.cos(a1), np.cos(a1), z(32)])
    s1a = cat([z(QK_NOPE), -np.sin(a1), z(ROT), z(32)])
    s2a = cat([z(QK_NOPE), z(ROT), np.sin(a1), z(32)])
    nope = cat([np.ones((seq, QK_NOPE)), z(LANES - QK_NOPE)])
    scale_a = (QK_NOPE + QK_ROPE) ** -0.5 * LOG2E
    taba = cat([(ca + nope) * scale_a, s1a * scale_a, s2a * scale_a, ca, s1a, s2a])
    cb = cat([np.cos(ar), np.cos(ar), np.cos(ac), np.cos(ac)] * 2)
    s1b = cat([-np.sin(ar), z(ROT), -np.sin(ac), z(ROT)] * 2)
    s2b = cat([z(ROT), np.sin(ar), z(ROT), np.sin(ac)] * 2)
    tabb = cat([cb, s1b, s2b])
    return jnp.asarray(taba, F32), jnp.asarray(tabb, F32)


def kernel(x, p, g_mix, w_in, g_qa, w_qb, g_kva, w_kvb, g_qn, g_kn, w_oa, w_ob, w_o,
           g_mlp, w_up, w_down, g_ple, w_ple_gate, w_ple, g_final):
    b, s, d = x.shape
    t = b * s
    x2 = x.reshape(t, d)
    taba, tabb = _rope_tables(s)
    seg = (jnp.arange(H_B * HD_B)[:, None] // HD_B == jnp.arange(H_B * HD_B)[None, :] // HD_B).astype(BF16)
    row = lambda g: g.reshape(1, -1)

    assert w_in.shape[0] == 1, "the block is built for the stated depth of one layer"
    wi = w_in[0].astype(BF16)
    o = 0
    parts = {}
    for name, w in (("qlat", Q_LORA), ("ckv", KV_LORA), ("kpe", QK_ROPE), ("qb", H_B * HD_B),
                    ("kb", KV_B * HD_B), ("vb", KV_B * HD_B), ("ga", D_MODEL), ("gb", D_MODEL)):
        parts[name] = wi[:, o:o + w]
        o += w
    zc = lambda w: jnp.zeros((d, w), wi.dtype)
    win = jnp.concatenate([parts["qlat"], parts["ckv"], zc(QK_NOPE), parts["kpe"],
                           zc(LANES - QK_NOPE - QK_ROPE), parts["qb"], parts["kb"], parts["vb"],
                           parts["ga"], parts["gb"]], axis=1)
    wqb = jnp.pad(w_qb[0].reshape(Q_LORA, H_A, QK_NOPE + QK_ROPE),
                  ((0, 0), (0, 0), (0, LANES - QK_NOPE - QK_ROPE))).reshape(Q_LORA, H_A * LANES).astype(BF16)
    wkv = w_kvb[0].reshape(KV_LORA, H_A, QK_NOPE + V_DIM_A)
    wk = jnp.pad(wkv[:, :, :QK_NOPE], ((0, 0), (0, 0), (0, LANES - QK_NOPE))).reshape(KV_LORA, H_A * LANES).astype(BF16)
    wvt = wkv[:, :, QK_NOPE:].reshape(KV_LORA, H_A * V_DIM_A).T.astype(BF16)

    qa, ka, vat, qb, kbm, vbt, ga, gb = _proj_call(
        x2, row(g_mix[0]), win, row(g_qa[0]), wqb, row(g_kva[0]), wk, wvt,
        row(jnp.tile(g_qn[0], H_B)), row(jnp.tile(g_kn[0], KV_B)), seg, taba, tabb, b, s)

    group = H_B // KV_B
    per_tile = LANES // V_DIM
    heads = tuple((0, hd * LANES, hd * LANES, hd * V_DIM, hd // per_tile) for hd in range(H_A))
    heads += tuple((1, (hd // 2) * LANES, (2 * (hd // group) + hd % 2) * LANES, (hd // group) * V_DIM,
                    hd // per_tile) for hd in range(H_B))
    oa, ob = _attn_call(qa.reshape(b, s, -1), ka.reshape(b, s, -1), vat,
                        qb.reshape(b, s, -1), kbm.reshape(b, s, -1), vbt, heads)

    out = _post_call(x2, oa.reshape(t, -1), ob.reshape(t, -1), ga, gb, p[0].reshape(t, -1),
                     w_oa[0].astype(BF16), w_ob[0].astype(BF16), w_o[0].astype(BF16), row(g_mlp[0]),
                     w_up[0].astype(BF16), w_down[0].astype(BF16), row(g_ple[0]),
                     w_ple_gate[0].astype(BF16), w_ple[0].astype(BF16), row(g_final))
    return out.reshape(b, s, d)
```

```python
import functools
import math

import jax
import jax.numpy as jnp
import numpy as np
from jax import lax
from jax.experimental import pallas as pl
from jax.experimental.pallas import tpu as pltpu

F32 = jnp.float32
BF16 = jnp.bfloat16

D_MODEL = 1024
GRID_W = 64
ROPE_THETA = 10000.0
EPS = 1e-6
H_A = 8
QK_NOPE = 64
QK_ROPE = 32
V_DIM_A = 64
Q_LORA = 256
KV_LORA = 128
H_B = 8
KV_B = 2
HD_B = 64
D_FF = 4 * D_MODEL
PLE_DIM = 256

LANES = 128
HALF = LANES // 2
ROT = QK_ROPE // 2
LOG2E = math.log2(math.e)

C_QLAT = 0
C_CKV = C_QLAT + Q_LORA
C_QB = C_CKV + KV_LORA + LANES
C_KV_B = C_QB + H_B * HD_B
C_GA = C_KV_B + 2 * KV_B * HD_B
C_GB = C_GA + D_MODEL
D_IN_R = C_GB + D_MODEL

TM_PROJ = 512
TQ_ATTN = 256
Q_TILES_PER_STEP = 2
KEY_CHUNK = 256
SCORE_AHEAD = 2
SUBLANES = 8
V_DIM = V_DIM_A
assert HD_B == V_DIM
TM_POST = 512
FF_CHUNK = 1024
VMEM_LIMIT = 56 * 1024 * 1024


def _rms(x, g):
    ms = jnp.mean(x * x, axis=-1, keepdims=True)
    return x * lax.rsqrt(ms + EPS) * g


def _rope(x, c, s1, s2):
    return x * c + pltpu.roll(x, LANES - ROT, 1) * s1 + pltpu.roll(x, ROT, 1) * s2


def _dot(a, b):
    return jnp.dot(a, b, preferred_element_type=F32)


def _dot_nt(a, b):
    return lax.dot_general(a, b, (((1,), (1,)), ((), ())), preferred_element_type=F32)


def _proj_kernel(x_ref, gmix_ref, win_ref, gqa_ref, wqb_ref, gkva_ref, wk_ref, wvt_ref,
                 gqn_ref, gkn_ref, seg_ref, taba_ref, tabb_ref,
                 qat_ref, ka_ref, vat_ref, qbt_ref, kbm_ref, vbt_ref, ga_ref, gb_ref):
    h = _rms(x_ref[...], gmix_ref[...]).astype(BF16)

    cq = _rms(_dot(h, win_ref[:, C_QLAT:C_QLAT + Q_LORA]), gqa_ref[...]).astype(BF16)
    cqa, s1qa, s2qa = (taba_ref[:, i * LANES:(i + 1) * LANES] for i in range(3))
    q = _dot(cq, wqb_ref[...])
    for hd in range(H_A):
        sl = slice(hd * LANES, (hd + 1) * LANES)
        qat_ref[0, sl, :] = _rope(q[:, sl], cqa, s1qa, s2qa).T.astype(BF16)

    zkv = _dot(h, win_ref[:, C_CKV:C_CKV + KV_LORA + LANES])
    ckv = _rms(zkv[:, :KV_LORA], gkva_ref[...]).astype(BF16)
    cka, s1ka, s2ka = (taba_ref[:, i * LANES:(i + 1) * LANES] for i in range(3, 6))
    kpe = _rope(zkv[:, KV_LORA:], cka, s1ka, s2ka)
    kn = _dot(ckv, wk_ref[...])
    for hd in range(H_A):
        sl = slice(hd * LANES, (hd + 1) * LANES)
        ka_ref[:, sl] = (kn[:, sl] + kpe).astype(BF16)
    vat_ref[0] = _dot_nt(wvt_ref[...], ckv).astype(BF16)

    cb, s1b, s2b = (tabb_ref[:, i * LANES:(i + 1) * LANES] for i in range(3))
    zq = _dot(h, win_ref[:, C_QB:C_QB + H_B * HD_B])
    ssq = _dot((zq * zq).astype(BF16), seg_ref[...])
    yq = zq * (lax.rsqrt(ssq * (1.0 / HD_B) + EPS) * (HD_B ** -0.5 * LOG2E)) * gqn_ref[...]
    for j in range(H_B * HD_B // LANES):
        sl = slice(j * LANES, (j + 1) * LANES)
        qbt_ref[0, sl, :] = _rope(yq[:, sl], cb, s1b, s2b).T.astype(BF16)

    zkvb = _dot(h, win_ref[:, C_KV_B:C_KV_B + 2 * LANES])
    zk = zkvb[:, :LANES]
    ssk = _dot((zk * zk).astype(BF16), seg_ref[0:LANES, 0:LANES])
    yk = zk * lax.rsqrt(ssk * (1.0 / HD_B) + EPS) * gkn_ref[...]
    kb = _rope(yk, cb, s1b, s2b)
    lane = lax.broadcasted_iota(jnp.int32, kb.shape, 1)
    lo = jnp.where(lane < HALF, kb, 0.0)
    hi = kb - lo
    kbm_ref[:, 0 * LANES:1 * LANES] = lo.astype(BF16)
    kbm_ref[:, 1 * LANES:2 * LANES] = pltpu.roll(lo, HALF, 1).astype(BF16)
    kbm_ref[:, 2 * LANES:3 * LANES] = pltpu.roll(hi, HALF, 1).astype(BF16)
    kbm_ref[:, 3 * LANES:4 * LANES] = hi.astype(BF16)
    vbt_ref[0] = zkvb[:, LANES:].T.astype(BF16)

    ga_ref[...] = _dot(h, win_ref[:, C_GA:C_GA + D_MODEL]).astype(BF16)
    gb_ref[...] = _dot(h, win_ref[:, C_GB:C_GB + D_MODEL]).astype(BF16)


def _const_spec(shape):
    return pl.BlockSpec(shape, lambda i: (0,) * len(shape), pipeline_mode=pl.Buffered(1))


def _proj_call(x2, gmix, win, gqa, wqb, gkva, wk, wvt, gqn, gkn, seg, taba, tabb, batch, seq):
    t = x2.shape[0]
    tm = TM_PROJ
    nseq = seq // tm
    row = lambda w: pl.BlockSpec((tm, w), lambda i: (i, 0))
    tab = lambda w: pl.BlockSpec((tm, w), lambda i: (i % nseq, 0))
    colT = lambda r: pl.BlockSpec((1, r, tm), lambda i: (i // nseq, 0, i % nseq))
    rows2d = lambda w: jax.ShapeDtypeStruct((t, w), BF16)
    colsT = lambda r: jax.ShapeDtypeStruct((batch, r, seq), BF16)
    consts = (gmix, win, gqa, wqb, gkva, wk, wvt, gqn, gkn, seg)
    return pl.pallas_call(
        _proj_kernel,
        grid=(t // tm,),
        in_specs=[row(D_MODEL)] + [_const_spec(c.shape) for c in consts]
                 + [tab(taba.shape[1]), tab(tabb.shape[1])],
        out_specs=[colT(H_A * LANES), row(H_A * LANES), colT(H_A * V_DIM_A),
                   colT(H_B * HD_B), row(4 * LANES), colT(KV_B * HD_B),
                   row(D_MODEL), row(D_MODEL)],
        out_shape=[colsT(H_A * LANES), rows2d(H_A * LANES), colsT(H_A * V_DIM_A),
                   colsT(H_B * HD_B), rows2d(4 * LANES), colsT(KV_B * HD_B),
                   rows2d(D_MODEL), rows2d(D_MODEL)],
        compiler_params=pltpu.CompilerParams(dimension_semantics=("parallel",),
                                             vmem_limit_bytes=VMEM_LIMIT),
        name="proj",
    )(x2, *consts, taba, tabb)


def _attn_kernel(qa_ref, ka_ref, vat_ref, qb_ref, kb_ref, vbt_ref, oa_ref, ob_ref,
                 st0_ref, st1_ref, st2_ref, pt_ref, *, heads):
    refs = ((qa_ref, ka_ref, vat_ref, oa_ref), (qb_ref, kb_ref, vbt_ref, ob_ref))
    st_refs = (st0_ref, st1_ref, st2_ref)
    per_tile = LANES // V_DIM
    n_chunks = ka_ref.shape[1] // KEY_CHUNK
    tq = TQ_ATTN

    def combine(a, b, op):
        return b if a is None else op(a, b)

    def scores(i, c):
        mixer, q_off, k_off = heads[i][:3]
        q_ref, k_ref = refs[mixer][:2]
        rows = slice(c * KEY_CHUNK, (c + 1) * KEY_CHUNK)
        q_cols = slice(heads[i][5] * tq, (heads[i][5] + 1) * tq)
        st = _dot(k_ref[0, rows, k_off:k_off + LANES], q_ref[0, q_off:q_off + LANES, q_cols])
        st_refs[i % len(st_refs)][rows, :] = st
        return jnp.max(st.reshape(KEY_CHUNK // SUBLANES, SUBLANES, tq), axis=0)

    acc = [None] * len(heads)
    l8 = [None] * len(heads)
    outs = {}

    def pv(i, c):
        mixer, _, _, v_off, tile, q_tile = heads[i]
        vt_ref, o_ref = refs[mixer][2:]
        rows = slice(c * KEY_CHUNK, (c + 1) * KEY_CHUNK)
        acc[i] = combine(acc[i], _dot(vt_ref[0, v_off:v_off + V_DIM, rows], pt_ref[(i * n_chunks + c) % 2]),
                         jnp.add)
        if c == n_chunks - 1:
            done = outs.setdefault((mixer, tile, q_tile), [])
            done.append(acc[i] / jnp.sum(l8[i], axis=0, keepdims=True))
            acc[i] = None
            if len(done) == per_tile:
                o_ref[0, q_tile * tq:(q_tile + 1) * tq, tile * LANES:(tile + 1) * LANES] = (
                    jnp.concatenate(done, axis=0).T.astype(BF16))

    m8 = [None] * len(heads)

    def scores_into(i, c):
        m8[i] = combine(m8[i], scores(i, c), jnp.maximum)

    for i in range(min(SCORE_AHEAD, len(heads))):
        for c in range(n_chunks):
            scores_into(i, c)
    prev = None
    for i in range(len(heads)):
        m = jnp.max(m8[i], axis=0, keepdims=True)
        for c in range(n_chunks):
            if i + SCORE_AHEAD < len(heads):
                scores_into(i + SCORE_AHEAD, c)
            if prev is not None:
                pv(*prev)
            rows = slice(c * KEY_CHUNK, (c + 1) * KEY_CHUNK)
            pt = jnp.exp2(st_refs[i % len(st_refs)][rows, :] - m)
            l8[i] = combine(l8[i], jnp.sum(pt.reshape(KEY_CHUNK // SUBLANES, SUBLANES, tq), axis=0), jnp.add)
            pt_ref[(i * n_chunks + c) % 2] = pt.astype(BF16)
            prev = (i, c)
    pv(*prev)


def _attn_call(qat, ka, vat, qbt, kb, vbt, heads):
    b, s, _ = ka.shape
    tq = TQ_ATTN * Q_TILES_PER_STEP
    heads = tuple(h + (qt,) for qt in range(Q_TILES_PER_STEP) for h in heads)
    qspec = lambda a: pl.BlockSpec((1, a.shape[1], tq), lambda i, j: (i, 0, j))
    kspec = lambda a: pl.BlockSpec((1, s, a.shape[2]), lambda i, j: (i, 0, 0))
    vspec = lambda a: pl.BlockSpec((1, a.shape[1], s), lambda i, j: (i, 0, 0))
    n_out = [sum(1 for h in heads if h[0] == mixer) * V_DIM // Q_TILES_PER_STEP for mixer in (0, 1)]
    return pl.pallas_call(
        functools.partial(_attn_kernel, heads=heads),
        grid=(b, s // tq),
        in_specs=[qspec(qat), kspec(ka), vspec(vat), qspec(qbt), kspec(kb), vspec(vbt)],
        out_specs=[pl.BlockSpec((1, tq, n), lambda i, j: (i, j, 0)) for n in n_out],
        out_shape=[jax.ShapeDtypeStruct((b, s, n), BF16) for n in n_out],
        scratch_shapes=[pltpu.VMEM((s, TQ_ATTN), F32)] * (SCORE_AHEAD + 1) + [
                        pltpu.VMEM((2, KEY_CHUNK, TQ_ATTN), BF16)],
        compiler_params=pltpu.CompilerParams(dimension_semantics=("parallel", "parallel"),
                                             vmem_limit_bytes=VMEM_LIMIT),
        name="attn",
    )(qat, ka, vat, qbt, kb, vbt)


def _post_kernel(x_ref, oa_ref, ob_ref, ga_ref, gb_ref, p_ref,
                 woa_ref, wob_ref, wo_ref, gmlp_ref, wup_ref, wdn_ref,
                 gple_ref, wpg_ref, wple_ref, gfin_ref, out_ref):
    ya = _dot(oa_ref[...], woa_ref[...])
    yb = _dot(ob_ref[...], wob_ref[...])
    merged = jax.nn.sigmoid(ga_ref[...].astype(F32)) * ya + jax.nn.sigmoid(gb_ref[...].astype(F32)) * yb
    x1 = x_ref[...] + _dot(merged.astype(BF16), wo_ref[...])

    h2 = _rms(x1, gmlp_ref[...]).astype(BF16)
    acc = jnp.zeros_like(x1)
    for c in range(D_FF // FF_CHUNK):
        sl = slice(c * FF_CHUNK, (c + 1) * FF_CHUNK)
        u = jnp.maximum(_dot(h2, wup_ref[:, sl]), 0.0)
        acc = acc + _dot((u * u).astype(BF16), wdn_ref[sl, :])
    x2 = x1 + acc

    h3 = _rms(x2, gple_ref[...]).astype(BF16)
    gate = jax.nn.sigmoid(_dot(h3, wpg_ref[...]))
    x3 = x2 + gate * _dot(p_ref[...].astype(BF16), wple_ref[...])
    out_ref[...] = _rms(x3, gfin_ref[...])


def _post_call(x2, oa, ob, ga, gb, p2, woa, wob, wo, gmlp, wup, wdn, gple, wpg, wple, gfin):
    t = x2.shape[0]
    tm = TM_POST
    row = lambda w: pl.BlockSpec((tm, w), lambda i: (i, 0))
    consts = (woa, wob, wo, gmlp, wup, wdn, gple, wpg, wple, gfin)
    return pl.pallas_call(
        _post_kernel,
        grid=(t // tm,),
        in_specs=[row(D_MODEL), row(oa.shape[1]), row(ob.shape[1]), row(D_MODEL), row(D_MODEL),
                  row(PLE_DIM)] + [_const_spec(c.shape) for c in consts],
        out_specs=row(D_MODEL),
        out_shape=jax.ShapeDtypeStruct((t, D_MODEL), F32),
        compiler_params=pltpu.CompilerParams(dimension_semantics=("parallel",),
                                             vmem_limit_bytes=VMEM_LIMIT),
        name="post",
    )(x2, oa, ob, ga, gb, p2, *consts)


def _rope_tables(seq):
    t = np.arange(seq)
    inv = ROPE_THETA ** (-np.arange(0, QK_ROPE, 2, dtype=np.float64) / QK_ROPE)
    ang = lambda pos: pos.astype(np.float64)[:, None] * inv[None, :]
    a1, ar, ac = ang(t), ang(t // GRID_W), ang(t % GRID_W)
    z = lambda w: np.zeros((seq, w), np.float64)
    cat = lambda parts: np.concatenate(parts, axis=1)
    ca = cat([z(QK_NOPE), np.cos(a1), np.cos(a1), z(32)])
    s1a = cat([z(QK_NOPE), -np.sin(a1), z(ROT), z(32)])
    s2a = cat([z(QK_NOPE), z(ROT), np.sin(a1), z(32)])
    nope = cat([np.ones((seq, QK_NOPE)), z(LANES - QK_NOPE)])
    scale_a = (QK_NOPE + QK_ROPE) ** -0.5 * LOG2E
    taba = cat([(ca + nope) * scale_a, s1a * scale_a, s2a * scale_a, ca, s1a, s2a])
    cb = cat([np.cos(ar), np.cos(ar), np.cos(ac), np.cos(ac)] * 2)
    s1b = cat([-np.sin(ar), z(ROT), -np.sin(ac), z(ROT)] * 2)
    s2b = cat([z(ROT), np.sin(ar), z(ROT), np.sin(ac)] * 2)
    tabb = cat([cb, s1b, s2b])
    return jnp.asarray(taba, F32), jnp.asarray(tabb, F32)


def kernel(x, p, g_mix, w_in, g_qa, w_qb, g_kva, w_kvb, g_qn, g_kn, w_oa, w_ob, w_o,
           g_mlp, w_up, w_down, g_ple, w_ple_gate, w_ple, g_final):
    b, s, d = x.shape
    t = b * s
    x2 = x.reshape(t, d)
    taba, tabb = _rope_tables(s)
    seg = (jnp.arange(H_B * HD_B)[:, None] // HD_B == jnp.arange(H_B * HD_B)[None, :] // HD_B).astype(BF16)
    row = lambda g: g.reshape(1, -1)

    assert w_in.shape[0] == 1, "the block is built for the stated depth of one layer"
    wi = w_in[0].astype(BF16)
    o = 0
    parts = {}
    for name, w in (("qlat", Q_LORA), ("ckv", KV_LORA), ("kpe", QK_ROPE), ("qb", H_B * HD_B),
                    ("kb", KV_B * HD_B), ("vb", KV_B * HD_B), ("ga", D_MODEL), ("gb", D_MODEL)):
        parts[name] = wi[:, o:o + w]
        o += w
    zc = lambda w: jnp.zeros((d, w), wi.dtype)
    win = jnp.concatenate([parts["qlat"], parts["ckv"], zc(QK_NOPE), parts["kpe"],
                           zc(LANES - QK_NOPE - QK_ROPE), parts["qb"], parts["kb"], parts["vb"],
                           parts["ga"], parts["gb"]], axis=1)
    wqb = jnp.pad(w_qb[0].reshape(Q_LORA, H_A, QK_NOPE + QK_ROPE),
                  ((0, 0), (0, 0), (0, LANES - QK_NOPE - QK_ROPE))).reshape(Q_LORA, H_A * LANES).astype(BF16)
    wkv = w_kvb[0].reshape(KV_LORA, H_A, QK_NOPE + V_DIM_A)
    wk = jnp.pad(wkv[:, :, :QK_NOPE], ((0, 0), (0, 0), (0, LANES - QK_NOPE))).reshape(KV_LORA, H_A * LANES).astype(BF16)
    wvt = wkv[:, :, QK_NOPE:].reshape(KV_LORA, H_A * V_DIM_A).T.astype(BF16)

    qat, ka, vat, qbt, kbm, vbt, ga, gb = _proj_call(
        x2, row(g_mix[0]), win, row(g_qa[0]), wqb, row(g_kva[0]), wk, wvt,
        row(jnp.tile(g_qn[0], H_B)), row(jnp.tile(g_kn[0], KV_B)), seg, taba, tabb, b, s)

    group = H_B // KV_B
    per_tile = LANES // V_DIM
    heads = tuple((0, hd * LANES, hd * LANES, hd * V_DIM, hd // per_tile) for hd in range(H_A))
    heads += tuple((1, (hd // 2) * LANES, (2 * (hd // group) + hd % 2) * LANES, (hd // group) * V_DIM,
                    hd // per_tile) for hd in range(H_B))
    oa, ob = _attn_call(qat, ka.reshape(b, s, -1), vat, qbt, kbm.reshape(b, s, -1), vbt, heads)

    out = _post_call(x2, oa.reshape(t, -1), ob.reshape(t, -1), ga, gb, p[0].reshape(t, -1),
                     w_oa[0].astype(BF16), w_ob[0].astype(BF16), w_o[0].astype(BF16), row(g_mlp[0]),
                     w_up[0].astype(BF16), w_down[0].astype(BF16), row(g_ple[0]),
                     w_ple_gate[0].astype(BF16), w_ple[0].astype(BF16), row(g_final))
    return out.reshape(b, s, d)
```

```python
import functools
import math

import jax
import jax.numpy as jnp
import numpy as np
from jax import lax
from jax.experimental import pallas as pl
from jax.experimental.pallas import tpu as pltpu

F32 = jnp.float32
BF16 = jnp.bfloat16

D_MODEL = 1024
GRID_W = 64
ROPE_THETA = 10000.0
EPS = 1e-6
H_A = 8
QK_NOPE = 64
QK_ROPE = 32
V_DIM_A = 64
Q_LORA = 256
KV_LORA = 128
H_B = 8
KV_B = 2
HD_B = 64
D_FF = 4 * D_MODEL
PLE_DIM = 256

LANES = 128
HALF = LANES // 2
ROT = QK_ROPE // 2
LOG2E = math.log2(math.e)

C_QLAT = 0
C_CKV = C_QLAT + Q_LORA
C_QB = C_CKV + KV_LORA + LANES
C_KV_B = C_QB + H_B * HD_B
C_GA = C_KV_B + 2 * KV_B * HD_B
C_GB = C_GA + D_MODEL
D_IN_R = C_GB + D_MODEL

TM_PROJ = 512
TQ_ATTN = 256
Q_TILES_PER_STEP = 2
KEY_CHUNK = 256
SCORE_CHUNK = 512
SCORE_AHEAD = 2
SUBLANES = 8
V_DIM = V_DIM_A
assert HD_B == V_DIM
TM_POST = 512
FF_CHUNK = 1024
VMEM_LIMIT = 56 * 1024 * 1024


def _rms(x, g):
    ms = jnp.mean(x * x, axis=-1, keepdims=True)
    return x * lax.rsqrt(ms + EPS) * g


def _rope(x, c, s1, s2):
    return x * c + pltpu.roll(x, LANES - ROT, 1) * s1 + pltpu.roll(x, ROT, 1) * s2


def _dot(a, b):
    return jnp.dot(a, b, preferred_element_type=F32)


def _dot_nt(a, b):
    return lax.dot_general(a, b, (((1,), (1,)), ((), ())), preferred_element_type=F32)


def _proj_kernel(x_ref, gmix_ref, win_ref, gqa_ref, wqb_ref, gkva_ref, wk_ref, wvt_ref,
                 gqn_ref, gkn_ref, seg_ref, taba_ref, tabb_ref,
                 qat_ref, ka_ref, vat_ref, qbt_ref, kbm_ref, vbt_ref, ga_ref, gb_ref):
    h = _rms(x_ref[...], gmix_ref[...]).astype(BF16)
    z = _dot(h, win_ref[:, 0:C_GA])

    cq = _rms(z[:, C_QLAT:C_QLAT + Q_LORA], gqa_ref[...]).astype(BF16)
    cqa, s1qa, s2qa = (taba_ref[:, i * LANES:(i + 1) * LANES] for i in range(3))
    q = _dot(cq, wqb_ref[...])
    for hd in range(H_A):
        sl = slice(hd * LANES, (hd + 1) * LANES)
        qat_ref[0, sl, :] = _rope(q[:, sl], cqa, s1qa, s2qa).T.astype(BF16)

    zkv = z[:, C_CKV:C_CKV + KV_LORA + LANES]
    ckv = _rms(zkv[:, :KV_LORA], gkva_ref[...]).astype(BF16)
    cka, s1ka, s2ka = (taba_ref[:, i * LANES:(i + 1) * LANES] for i in range(3, 6))
    kpe = _rope(zkv[:, KV_LORA:], cka, s1ka, s2ka)
    kn = _dot(ckv, wk_ref[...])
    for hd in range(H_A):
        sl = slice(hd * LANES, (hd + 1) * LANES)
        ka_ref[:, sl] = (kn[:, sl] + kpe).astype(BF16)
    vat_ref[0] = _dot_nt(wvt_ref[...], ckv).astype(BF16)

    cb, s1b, s2b = (tabb_ref[:, i * LANES:(i + 1) * LANES] for i in range(3))
    zq = z[:, C_QB:C_QB + H_B * HD_B]
    ssq = _dot((zq * zq).astype(BF16), seg_ref[...])
    yq = zq * (lax.rsqrt(ssq * (1.0 / HD_B) + EPS) * (HD_B ** -0.5 * LOG2E)) * gqn_ref[...]
    for j in range(H_B * HD_B // LANES):
        sl = slice(j * LANES, (j + 1) * LANES)
        qbt_ref[0, sl, :] = _rope(yq[:, sl], cb, s1b, s2b).T.astype(BF16)

    zkvb = z[:, C_KV_B:C_KV_B + 2 * LANES]
    zk = zkvb[:, :LANES]
    ssk = _dot((zk * zk).astype(BF16), seg_ref[0:LANES, 0:LANES])
    yk = zk * lax.rsqrt(ssk * (1.0 / HD_B) + EPS) * gkn_ref[...]
    kb = _rope(yk, cb, s1b, s2b)
    lane = lax.broadcasted_iota(jnp.int32, kb.shape, 1)
    lo = jnp.where(lane < HALF, kb, 0.0)
    hi = kb - lo
    kbm_ref[:, 0 * LANES:1 * LANES] = lo.astype(BF16)
    kbm_ref[:, 1 * LANES:2 * LANES] = pltpu.roll(lo, HALF, 1).astype(BF16)
    kbm_ref[:, 2 * LANES:3 * LANES] = pltpu.roll(hi, HALF, 1).astype(BF16)
    kbm_ref[:, 3 * LANES:4 * LANES] = hi.astype(BF16)
    vbt_ref[0] = zkvb[:, LANES:].T.astype(BF16)

    gates = _dot(h, win_ref[:, C_GA:C_GA + 2 * D_MODEL])
    ga_ref[...] = gates[:, :D_MODEL].astype(BF16)
    gb_ref[...] = gates[:, D_MODEL:].astype(BF16)


def _const_spec(shape):
    return pl.BlockSpec(shape, lambda i: (0,) * len(shape), pipeline_mode=pl.Buffered(1))


def _proj_call(x2, gmix, win, gqa, wqb, gkva, wk, wvt, gqn, gkn, seg, taba, tabb, batch, seq):
    t = x2.shape[0]
    tm = TM_PROJ
    nseq = seq // tm
    row = lambda w: pl.BlockSpec((tm, w), lambda i: (i, 0))
    tab = lambda w: pl.BlockSpec((tm, w), lambda i: (i % nseq, 0))
    colT = lambda r: pl.BlockSpec((1, r, tm), lambda i: (i // nseq, 0, i % nseq))
    rows2d = lambda w: jax.ShapeDtypeStruct((t, w), BF16)
    colsT = lambda r: jax.ShapeDtypeStruct((batch, r, seq), BF16)
    consts = (gmix, win, gqa, wqb, gkva, wk, wvt, gqn, gkn, seg)
    return pl.pallas_call(
        _proj_kernel,
        grid=(t // tm,),
        in_specs=[row(D_MODEL)] + [_const_spec(c.shape) for c in consts]
                 + [tab(taba.shape[1]), tab(tabb.shape[1])],
        out_specs=[colT(H_A * LANES), row(H_A * LANES), colT(H_A * V_DIM_A),
                   colT(H_B * HD_B), row(4 * LANES), colT(KV_B * HD_B),
                   row(D_MODEL), row(D_MODEL)],
        out_shape=[colsT(H_A * LANES), rows2d(H_A * LANES), colsT(H_A * V_DIM_A),
                   colsT(H_B * HD_B), rows2d(4 * LANES), colsT(KV_B * HD_B),
                   rows2d(D_MODEL), rows2d(D_MODEL)],
        compiler_params=pltpu.CompilerParams(dimension_semantics=("parallel",),
                                             vmem_limit_bytes=VMEM_LIMIT),
        name="proj",
    )(x2, *consts, taba, tabb)


def _attn_kernel(qa_ref, ka_ref, vat_ref, qb_ref, kb_ref, vbt_ref, oa_ref, ob_ref,
                 st0_ref, st1_ref, st2_ref, pt_ref, *, heads):
    refs = ((qa_ref, ka_ref, vat_ref, oa_ref), (qb_ref, kb_ref, vbt_ref, ob_ref))
    st_refs = (st0_ref, st1_ref, st2_ref)
    per_tile = LANES // V_DIM
    n_chunks = ka_ref.shape[1] // KEY_CHUNK
    tq = TQ_ATTN

    def combine(a, b, op):
        return b if a is None else op(a, b)

    def scores(i, c):
        mixer, q_off, k_off = heads[i][:3]
        q_ref, k_ref = refs[mixer][:2]
        rows = slice(c * SCORE_CHUNK, (c + 1) * SCORE_CHUNK)
        q_cols = slice(heads[i][5] * tq, (heads[i][5] + 1) * tq)
        st = _dot(k_ref[0, rows, k_off:k_off + LANES], q_ref[0, q_off:q_off + LANES, q_cols])
        st_refs[i % len(st_refs)][rows, :] = st
        return jnp.max(st.reshape(SCORE_CHUNK // SUBLANES, SUBLANES, tq), axis=0)

    acc = [None] * len(heads)
    l8 = [None] * len(heads)
    outs = {}

    def pv(i, c):
        mixer, _, _, v_off, tile, q_tile = heads[i]
        vt_ref, o_ref = refs[mixer][2:]
        rows = slice(c * KEY_CHUNK, (c + 1) * KEY_CHUNK)
        acc[i] = combine(acc[i], _dot(vt_ref[0, v_off:v_off + V_DIM, rows], pt_ref[(i * n_chunks + c) % 2]),
                         jnp.add)
        if c == n_chunks - 1:
            done = outs.setdefault((mixer, tile, q_tile), [])
            done.append(acc[i] / jnp.sum(l8[i], axis=0, keepdims=True))
            acc[i] = None
            if len(done) == per_tile:
                o_ref[0, q_tile * tq:(q_tile + 1) * tq, tile * LANES:(tile + 1) * LANES] = (
                    jnp.concatenate(done, axis=0).T.astype(BF16))

    m8 = [None] * len(heads)

    def scores_into(i, c):
        m8[i] = combine(m8[i], scores(i, c), jnp.maximum)

    score_every = SCORE_CHUNK // KEY_CHUNK
    for i in range(min(SCORE_AHEAD, len(heads))):
        for c in range(n_chunks // score_every):
            scores_into(i, c)
    prev = None
    for i in range(len(heads)):
        m = jnp.max(m8[i], axis=0, keepdims=True)
        for c in range(n_chunks):
            if i + SCORE_AHEAD < len(heads) and c % score_every == 0:
                scores_into(i + SCORE_AHEAD, c // score_every)
            if prev is not None:
                pv(*prev)
            rows = slice(c * KEY_CHUNK, (c + 1) * KEY_CHUNK)
            pt = jnp.exp2(st_refs[i % len(st_refs)][rows, :] - m)
            l8[i] = combine(l8[i], jnp.sum(pt.reshape(KEY_CHUNK // SUBLANES, SUBLANES, tq), axis=0), jnp.add)
            pt_ref[(i * n_chunks + c) % 2] = pt.astype(BF16)
            prev = (i, c)
    pv(*prev)


def _attn_call(qat, ka, vat, qbt, kb, vbt, heads):
    b, s, _ = ka.shape
    tq = TQ_ATTN * Q_TILES_PER_STEP
    heads = tuple(h + (qt,) for qt in range(Q_TILES_PER_STEP) for h in heads)
    qspec = lambda a: pl.BlockSpec((1, a.shape[1], tq), lambda i, j: (i, 0, j))
    kspec = lambda a: pl.BlockSpec((1, s, a.shape[2]), lambda i, j: (i, 0, 0))
    vspec = lambda a: pl.BlockSpec((1, a.shape[1], s), lambda i, j: (i, 0, 0))
    n_out = [sum(1 for h in heads if h[0] == mixer) * V_DIM // Q_TILES_PER_STEP for mixer in (0, 1)]
    return pl.pallas_call(
        functools.partial(_attn_kernel, heads=heads),
        grid=(b, s // tq),
        in_specs=[qspec(qat), kspec(ka), vspec(vat), qspec(qbt), kspec(kb), vspec(vbt)],
        out_specs=[pl.BlockSpec((1, tq, n), lambda i, j: (i, j, 0)) for n in n_out],
        out_shape=[jax.ShapeDtypeStruct((b, s, n), BF16) for n in n_out],
        scratch_shapes=[pltpu.VMEM((s, TQ_ATTN), F32)] * (SCORE_AHEAD + 1) + [
                        pltpu.VMEM((2, KEY_CHUNK, TQ_ATTN), BF16)],
        compiler_params=pltpu.CompilerParams(dimension_semantics=("parallel", "parallel"),
                                             vmem_limit_bytes=VMEM_LIMIT),
        name="attn",
    )(qat, ka, vat, qbt, kb, vbt)


def _post_kernel(x_ref, oa_ref, ob_ref, ga_ref, gb_ref, p_ref,
                 woa_ref, wob_ref, wo_ref, gmlp_ref, wup_ref, wdn_ref,
                 gple_ref, wpg_ref, wple_ref, gfin_ref, out_ref):
    ya = _dot(oa_ref[...], woa_ref[...])
    yb = _dot(ob_ref[...], wob_ref[...])
    merged = jax.nn.sigmoid(ga_ref[...].astype(F32)) * ya + jax.nn.sigmoid(gb_ref[...].astype(F32)) * yb
    x1 = x_ref[...] + _dot(merged.astype(BF16), wo_ref[...])

    h2 = _rms(x1, gmlp_ref[...]).astype(BF16)
    acc = jnp.zeros_like(x1)
    for c in range(D_FF // FF_CHUNK):
        sl = slice(c * FF_CHUNK, (c + 1) * FF_CHUNK)
        u = jnp.maximum(_dot(h2, wup_ref[:, sl]), 0.0)
        acc = acc + _dot((u * u).astype(BF16), wdn_ref[sl, :])
    x2 = x1 + acc

    h3 = _rms(x2, gple_ref[...]).astype(BF16)
    gate = jax.nn.sigmoid(_dot(h3, wpg_ref[...]))
    x3 = x2 + gate * _dot(p_ref[...].astype(BF16), wple_ref[...])
    out_ref[...] = _rms(x3, gfin_ref[...])


def _post_call(x2, oa, ob, ga, gb, p2, woa, wob, wo, gmlp, wup, wdn, gple, wpg, wple, gfin):
    t = x2.shape[0]
    tm = TM_POST
    row = lambda w: pl.BlockSpec((tm, w), lambda i: (i, 0))
    consts = (woa, wob, wo, gmlp, wup, wdn, gple, wpg, wple, gfin)
    return pl.pallas_call(
        _post_kernel,
        grid=(t // tm,),
        in_specs=[row(D_MODEL), row(oa.shape[1]), row(ob.shape[1]), row(D_MODEL), row(D_MODEL),
                  row(PLE_DIM)] + [_const_spec(c.shape) for c in consts],
        out_specs=row(D_MODEL),
        out_shape=jax.ShapeDtypeStruct((t, D_MODEL), F32),
        compiler_params=pltpu.CompilerParams(dimension_semantics=("parallel",),
                                             vmem_limit_bytes=VMEM_LIMIT),
        name="post",
    )(x2, oa, ob, ga, gb, p2, *consts)


def _rope_tables(seq):
    t = np.arange(seq)
    inv = ROPE_THETA ** (-np.arange(0, QK_ROPE, 2, dtype=np.float64) / QK_ROPE)
    ang = lambda pos: pos.astype(np.float64)[:, None] * inv[None, :]
    a1, ar, ac = ang(t), ang(t // GRID_W), ang(t % GRID_W)
    z = lambda w: np.zeros((seq, w), np.float64)
    cat = lambda parts: np.concatenate(parts, axis=1)
    ca = cat([z(QK_NOPE), np.cos(a1), np.cos(a1), z(32)])
    s1a = cat([z(QK_NOPE), -np.sin(a1), z(ROT), z(32)])
    s2a = cat([z(QK_NOPE), z(ROT), np.sin(a1), z(32)])
    nope = cat([np.ones((seq, QK_NOPE)), z(LANES - QK_NOPE)])
    scale_a = (QK_NOPE + QK_ROPE) ** -0.5 * LOG2E
    taba = cat([(ca + nope) * scale_a, s1a * scale_a, s2a * scale_a, ca, s1a, s2a])
    cb = cat([np.cos(ar), np.cos(ar), np.cos(ac), np.cos(ac)] * 2)
    s1b = cat([-np.sin(ar), z(ROT), -np.sin(ac), z(ROT)] * 2)
    s2b = cat([z(ROT), np.sin(ar), z(ROT), np.sin(ac)] * 2)
    tabb = cat([cb, s1b, s2b])
    return jnp.asarray(taba, F32), jnp.asarray(tabb, F32)


def kernel(x, p, g_mix, w_in, g_qa, w_qb, g_kva, w_kvb, g_qn, g_kn, w_oa, w_ob, w_o,
           g_mlp, w_up, w_down, g_ple, w_ple_gate, w_ple, g_final):
    b, s, d = x.shape
    t = b * s
    x2 = x.reshape(t, d)
    taba, tabb = _rope_tables(s)
    seg = (jnp.arange(H_B * HD_B)[:, None] // HD_B == jnp.arange(H_B * HD_B)[None, :] // HD_B).astype(BF16)
    row = lambda g: g.reshape(1, -1)

    assert w_in.shape[0] == 1, "the block is built for the stated depth of one layer"
    wi = w_in[0].astype(BF16)
    o = 0
    parts = {}
    for name, w in (("qlat", Q_LORA), ("ckv", KV_LORA), ("kpe", QK_ROPE), ("qb", H_B * HD_B),
                    ("kb", KV_B * HD_B), ("vb", KV_B * HD_B), ("ga", D_MODEL), ("gb", D_MODEL)):
        parts[name] = wi[:, o:o + w]
        o += w
    zc = lambda w: jnp.zeros((d, w), wi.dtype)
    win = jnp.concatenate([parts["qlat"], parts["ckv"], zc(QK_NOPE), parts["kpe"],
                           zc(LANES - QK_NOPE - QK_ROPE), parts["qb"], parts["kb"], parts["vb"],
                           parts["ga"], parts["gb"]], axis=1)
    wqb = jnp.pad(w_qb[0].reshape(Q_LORA, H_A, QK_NOPE + QK_ROPE),
                  ((0, 0), (0, 0), (0, LANES - QK_NOPE - QK_ROPE))).reshape(Q_LORA, H_A * LANES).astype(BF16)
    wkv = w_kvb[0].reshape(KV_LORA, H_A, QK_NOPE + V_DIM_A)
    wk = jnp.pad(wkv[:, :, :QK_NOPE], ((0, 0), (0, 0), (0, LANES - QK_NOPE))).reshape(KV_LORA, H_A * LANES).astype(BF16)
    wvt = wkv[:, :, QK_NOPE:].reshape(KV_LORA, H_A * V_DIM_A).T.astype(BF16)

    qat, ka, vat, qbt, kbm, vbt, ga, gb = _proj_call(
        x2, row(g_mix[0]), win, row(g_qa[0]), wqb, row(g_kva[0]), wk, wvt,
        row(jnp.tile(g_qn[0], H_B)), row(jnp.tile(g_kn[0], KV_B)), seg, taba, tabb, b, s)

    group = H_B // KV_B
    per_tile = LANES // V_DIM
    heads = tuple((0, hd * LANES, hd * LANES, hd * V_DIM, hd // per_tile) for hd in range(H_A))
    heads += tuple((1, (hd // 2) * LANES, (2 * (hd // group) + hd % 2) * LANES, (hd // group) * V_DIM,
                    hd // per_tile) for hd in range(H_B))
    oa, ob = _attn_call(qat, ka.reshape(b, s, -1), vat, qbt, kbm.reshape(b, s, -1), vbt, heads)

    out = _post_call(x2, oa.reshape(t, -1), ob.reshape(t, -1), ga, gb, p[0].reshape(t, -1),
                     w_oa[0].astype(BF16), w_ob[0].astype(BF16), w_o[0].astype(BF16), row(g_mlp[0]),
                     w_up[0].astype(BF16), w_down[0].astype(BF16), row(g_ple[0]),
                     w_ple_gate[0].astype(BF16), w_ple[0].astype(BF16), row(g_final))
    return out.reshape(b, s, d)
```

```python
import functools
import math

import jax
import jax.numpy as jnp
import numpy as np
from jax import lax
from jax.experimental import pallas as pl
from jax.experimental.pallas import tpu as pltpu

F32 = jnp.float32
BF16 = jnp.bfloat16

D_MODEL = 1024
GRID_W = 64
ROPE_THETA = 10000.0
EPS = 1e-6
H_A = 8
QK_NOPE = 64
QK_ROPE = 32
V_DIM_A = 64
Q_LORA = 256
KV_LORA = 128
H_B = 8
KV_B = 2
HD_B = 64
D_FF = 4 * D_MODEL
PLE_DIM = 256

LANES = 128
HALF = LANES // 2
ROT = QK_ROPE // 2
LOG2E = math.log2(math.e)

C_QLAT = 0
C_CKV = C_QLAT + Q_LORA
C_QB = C_CKV + KV_LORA + LANES
C_KV_B = C_QB + H_B * HD_B
C_GA = C_KV_B + 2 * KV_B * HD_B

TM_PROJ = 512
TQ_ATTN = 256
Q_TILES_PER_STEP = 2
KEY_CHUNK = 256
SCORE_CHUNK = 512
SCORE_AHEAD = 2
SUBLANES = 8
V_DIM = V_DIM_A
assert HD_B == V_DIM
TM_POST = 512
FF_CHUNK = 1024
VMEM_LIMIT = 56 * 1024 * 1024


def _rms(x, g):
    ms = jnp.mean(x * x, axis=-1, keepdims=True)
    return x * lax.rsqrt(ms + EPS) * g


def _rope(x, c, s1, s2):
    return x * c + pltpu.roll(x, LANES - ROT, 1) * s1 + pltpu.roll(x, ROT, 1) * s2


def _dot(a, b):
    return jnp.dot(a, b, preferred_element_type=F32)


def _dot_nt(a, b):
    return lax.dot_general(a, b, (((1,), (1,)), ((), ())), preferred_element_type=F32)


def _proj_kernel(x_ref, gmix_ref, win_ref, wgate_ref, gqa_ref, wqb_ref, gkva_ref, wk_ref, wvt_ref,
                 gqn_ref, gkn_ref, seg_ref, taba_ref, tabb_ref,
                 qat_ref, ka_ref, vat_ref, qbt_ref, kbm_ref, vbt_ref, ga_ref, gb_ref):
    h = _rms(x_ref[...], gmix_ref[...]).astype(BF16)
    z = _dot(h, win_ref[...])

    cq = _rms(z[:, C_QLAT:C_QLAT + Q_LORA], gqa_ref[...]).astype(BF16)
    cqa, s1qa, s2qa = (taba_ref[:, i * LANES:(i + 1) * LANES] for i in range(3))
    q = _dot(cq, wqb_ref[...])
    for hd in range(H_A):
        sl = slice(hd * LANES, (hd + 1) * LANES)
        qat_ref[0, sl, :] = _rope(q[:, sl], cqa, s1qa, s2qa).T.astype(BF16)

    zkv = z[:, C_CKV:C_CKV + KV_LORA + LANES]
    ckv = _rms(zkv[:, :KV_LORA], gkva_ref[...]).astype(BF16)
    cka, s1ka, s2ka = (taba_ref[:, i * LANES:(i + 1) * LANES] for i in range(3, 6))
    kpe = _rope(zkv[:, KV_LORA:], cka, s1ka, s2ka)
    kn = _dot(ckv, wk_ref[...])
    for hd in range(H_A):
        sl = slice(hd * LANES, (hd + 1) * LANES)
        ka_ref[:, sl] = (kn[:, sl] + kpe).astype(BF16)
    vat_ref[0] = _dot_nt(wvt_ref[...], ckv).astype(BF16)

    cb, s1b, s2b = (tabb_ref[:, i * LANES:(i + 1) * LANES] for i in range(3))
    zq = z[:, C_QB:C_QB + H_B * HD_B]
    ssq = _dot((zq * zq).astype(BF16), seg_ref[...])
    yq = zq * (lax.rsqrt(ssq * (1.0 / HD_B) + EPS) * (HD_B ** -0.5 * LOG2E)) * gqn_ref[...]
    for j in range(H_B * HD_B // LANES):
        sl = slice(j * LANES, (j + 1) * LANES)
        qbt_ref[0, sl, :] = _rope(yq[:, sl], cb, s1b, s2b).T.astype(BF16)

    zkvb = z[:, C_KV_B:C_KV_B + 2 * LANES]
    zk = zkvb[:, :LANES]
    ssk = _dot((zk * zk).astype(BF16), seg_ref[0:LANES, 0:LANES])
    yk = zk * lax.rsqrt(ssk * (1.0 / HD_B) + EPS) * gkn_ref[...]
    kb = _rope(yk, cb, s1b, s2b)
    lane = lax.broadcasted_iota(jnp.int32, kb.shape, 1)
    lo = jnp.where(lane < HALF, kb, 0.0)
    hi = kb - lo
    kbm_ref[:, 0 * LANES:1 * LANES] = lo.astype(BF16)
    kbm_ref[:, 1 * LANES:2 * LANES] = pltpu.roll(lo, HALF, 1).astype(BF16)
    kbm_ref[:, 2 * LANES:3 * LANES] = pltpu.roll(hi, HALF, 1).astype(BF16)
    kbm_ref[:, 3 * LANES:4 * LANES] = hi.astype(BF16)
    vbt_ref[0] = zkvb[:, LANES:].T.astype(BF16)

    gates = _dot(h, wgate_ref[...])
    ga_ref[...] = gates[:, :D_MODEL].astype(BF16)
    gb_ref[...] = gates[:, D_MODEL:].astype(BF16)


def _const_spec(shape):
    return pl.BlockSpec(shape, lambda i: (0,) * len(shape), pipeline_mode=pl.Buffered(1))


def _proj_call(x2, gmix, win, wgate, gqa, wqb, gkva, wk, wvt, gqn, gkn, seg, taba, tabb, batch, seq):
    t = x2.shape[0]
    tm = TM_PROJ
    nseq = seq // tm
    row = lambda w: pl.BlockSpec((tm, w), lambda i: (i, 0))
    tab = lambda w: pl.BlockSpec((tm, w), lambda i: (i % nseq, 0))
    colT = lambda r: pl.BlockSpec((1, r, tm), lambda i: (i // nseq, 0, i % nseq))
    rows2d = lambda w: jax.ShapeDtypeStruct((t, w), BF16)
    colsT = lambda r: jax.ShapeDtypeStruct((batch, r, seq), BF16)
    consts = (gmix, win, wgate, gqa, wqb, gkva, wk, wvt, gqn, gkn, seg)
    return pl.pallas_call(
        _proj_kernel,
        grid=(t // tm,),
        in_specs=[row(D_MODEL)] + [_const_spec(c.shape) for c in consts]
                 + [tab(taba.shape[1]), tab(tabb.shape[1])],
        out_specs=[colT(H_A * LANES), row(H_A * LANES), colT(H_A * V_DIM_A),
                   colT(H_B * HD_B), row(4 * LANES), colT(KV_B * HD_B),
                   row(D_MODEL), row(D_MODEL)],
        out_shape=[colsT(H_A * LANES), rows2d(H_A * LANES), colsT(H_A * V_DIM_A),
                   colsT(H_B * HD_B), rows2d(4 * LANES), colsT(KV_B * HD_B),
                   rows2d(D_MODEL), rows2d(D_MODEL)],
        compiler_params=pltpu.CompilerParams(dimension_semantics=("parallel",),
                                             vmem_limit_bytes=VMEM_LIMIT),
        name="proj",
    )(x2, *consts, taba, tabb)


def _attn_kernel(qa_ref, ka_ref, vat_ref, qb_ref, kb_ref, vbt_ref, oa_ref, ob_ref,
                 st0_ref, st1_ref, st2_ref, pt_ref, *, heads):
    refs = ((qa_ref, ka_ref, vat_ref, oa_ref), (qb_ref, kb_ref, vbt_ref, ob_ref))
    st_refs = (st0_ref, st1_ref, st2_ref)
    per_tile = LANES // V_DIM
    n_chunks = ka_ref.shape[1] // KEY_CHUNK
    tq = TQ_ATTN

    def combine(a, b, op):
        return b if a is None else op(a, b)

    def scores(i, c):
        mixer, q_off, k_off = heads[i][:3]
        q_ref, k_ref = refs[mixer][:2]
        rows = slice(c * SCORE_CHUNK, (c + 1) * SCORE_CHUNK)
        q_cols = slice(heads[i][5] * tq, (heads[i][5] + 1) * tq)
        st = _dot(k_ref[0, rows, k_off:k_off + LANES], q_ref[0, q_off:q_off + LANES, q_cols])
        st_refs[i % len(st_refs)][rows, :] = st
        return jnp.max(st.reshape(SCORE_CHUNK // SUBLANES, SUBLANES, tq), axis=0)

    acc = [None] * len(heads)
    l8 = [None] * len(heads)
    outs = {}

    def pv(i, c):
        mixer, _, _, v_off, tile, q_tile = heads[i]
        vt_ref, o_ref = refs[mixer][2:]
        rows = slice(c * KEY_CHUNK, (c + 1) * KEY_CHUNK)
        acc[i] = combine(acc[i], _dot(vt_ref[0, v_off:v_off + V_DIM, rows], pt_ref[(i * n_chunks + c) % 2]),
                         jnp.add)
        if c == n_chunks - 1:
            done = outs.setdefault((mixer, tile, q_tile), [])
            done.append(acc[i] / jnp.sum(l8[i], axis=0, keepdims=True))
            acc[i] = None
            if len(done) == per_tile:
                o_ref[0, q_tile * tq:(q_tile + 1) * tq, tile * LANES:(tile + 1) * LANES] = (
                    jnp.concatenate(done, axis=0).T.astype(BF16))

    m8 = [None] * len(heads)

    def scores_into(i, c):
        m8[i] = combine(m8[i], scores(i, c), jnp.maximum)

    score_every = SCORE_CHUNK // KEY_CHUNK
    for i in range(min(SCORE_AHEAD, len(heads))):
        for c in range(n_chunks // score_every):
            scores_into(i, c)
    prev = None
    for i in range(len(heads)):
        m = jnp.max(m8[i], axis=0, keepdims=True)
        for c in range(n_chunks):
            if i + SCORE_AHEAD < len(heads) and c % score_every == 0:
                scores_into(i + SCORE_AHEAD, c // score_every)
            if prev is not None:
                pv(*prev)
            rows = slice(c * KEY_CHUNK, (c + 1) * KEY_CHUNK)
            pt = jnp.exp2(st_refs[i % len(st_refs)][rows, :] - m)
            l8[i] = combine(l8[i], jnp.sum(pt.reshape(KEY_CHUNK // SUBLANES, SUBLANES, tq), axis=0), jnp.add)
            pt_ref[(i * n_chunks + c) % 2] = pt.astype(BF16)
            prev = (i, c)
    pv(*prev)


def _attn_call(qat, ka, vat, qbt, kb, vbt, heads):
    b, s, _ = ka.shape
    tq = TQ_ATTN * Q_TILES_PER_STEP
    heads = tuple(h + (qt,) for qt in range(Q_TILES_PER_STEP) for h in heads)
    qspec = lambda a: pl.BlockSpec((1, a.shape[1], tq), lambda i, j: (i, 0, j))
    kspec = lambda a: pl.BlockSpec((1, s, a.shape[2]), lambda i, j: (i, 0, 0))
    vspec = lambda a: pl.BlockSpec((1, a.shape[1], s), lambda i, j: (i, 0, 0))
    n_out = [sum(1 for h in heads if h[0] == mixer) * V_DIM // Q_TILES_PER_STEP for mixer in (0, 1)]
    return pl.pallas_call(
        functools.partial(_attn_kernel, heads=heads),
        grid=(b, s // tq),
        in_specs=[qspec(qat), kspec(ka), vspec(vat), qspec(qbt), kspec(kb), vspec(vbt)],
        out_specs=[pl.BlockSpec((1, tq, n), lambda i, j: (i, j, 0)) for n in n_out],
        out_shape=[jax.ShapeDtypeStruct((b, s, n), BF16) for n in n_out],
        scratch_shapes=[pltpu.VMEM((s, TQ_ATTN), F32)] * (SCORE_AHEAD + 1) + [
                        pltpu.VMEM((2, KEY_CHUNK, TQ_ATTN), BF16)],
        compiler_params=pltpu.CompilerParams(dimension_semantics=("parallel", "parallel"),
                                             vmem_limit_bytes=VMEM_LIMIT),
        name="attn",
    )(qat, ka, vat, qbt, kb, vbt)


def _post_kernel(x_ref, oa_ref, ob_ref, ga_ref, gb_ref, p_ref,
                 woa_ref, wob_ref, wo_ref, gmlp_ref, wup_ref, wdn_ref,
                 gple_ref, wpg_ref, wple_ref, gfin_ref, out_ref):
    ya = _dot(oa_ref[...], woa_ref[...])
    yb = _dot(ob_ref[...], wob_ref[...])
    merged = jax.nn.sigmoid(ga_ref[...].astype(F32)) * ya + jax.nn.sigmoid(gb_ref[...].astype(F32)) * yb
    x1 = x_ref[...] + _dot(merged.astype(BF16), wo_ref[...])

    h2 = _rms(x1, gmlp_ref[...]).astype(BF16)
    acc = jnp.zeros_like(x1)
    for c in range(D_FF // FF_CHUNK):
        sl = slice(c * FF_CHUNK, (c + 1) * FF_CHUNK)
        u = jnp.maximum(_dot(h2, wup_ref[:, sl]), 0.0)
        acc = acc + _dot((u * u).astype(BF16), wdn_ref[sl, :])
    x2 = x1 + acc

    h3 = _rms(x2, gple_ref[...]).astype(BF16)
    gate = jax.nn.sigmoid(_dot(h3, wpg_ref[...]))
    x3 = x2 + gate * _dot(p_ref[...].astype(BF16), wple_ref[...])
    out_ref[...] = _rms(x3, gfin_ref[...])


def _post_call(x2, oa, ob, ga, gb, p2, woa, wob, wo, gmlp, wup, wdn, gple, wpg, wple, gfin):
    t = x2.shape[0]
    tm = TM_POST
    row = lambda w: pl.BlockSpec((tm, w), lambda i: (i, 0))
    consts = (woa, wob, wo, gmlp, wup, wdn, gple, wpg, wple, gfin)
    return pl.pallas_call(
        _post_kernel,
        grid=(t // tm,),
        in_specs=[row(D_MODEL), row(oa.shape[1]), row(ob.shape[1]), row(D_MODEL), row(D_MODEL),
                  row(PLE_DIM)] + [_const_spec(c.shape) for c in consts],
        out_specs=row(D_MODEL),
        out_shape=jax.ShapeDtypeStruct((t, D_MODEL), F32),
        compiler_params=pltpu.CompilerParams(dimension_semantics=("parallel",),
                                             vmem_limit_bytes=VMEM_LIMIT),
        name="post",
    )(x2, oa, ob, ga, gb, p2, *consts)


def _rope_tables(seq):
    t = np.arange(seq)
    inv = ROPE_THETA ** (-np.arange(0, QK_ROPE, 2, dtype=np.float64) / QK_ROPE)
    ang = lambda pos: pos.astype(np.float64)[:, None] * inv[None, :]
    a1, ar, ac = ang(t), ang(t // GRID_W), ang(t % GRID_W)
    z = lambda w: np.zeros((seq, w), np.float64)
    cat = lambda parts: np.concatenate(parts, axis=1)
    ca = cat([z(QK_NOPE), np.cos(a1), np.cos(a1), z(32)])
    s1a = cat([z(QK_NOPE), -np.sin(a1), z(ROT), z(32)])
    s2a = cat([z(QK_NOPE), z(ROT), np.sin(a1), z(32)])
    nope = cat([np.ones((seq, QK_NOPE)), z(LANES - QK_NOPE)])
    scale_a = (QK_NOPE + QK_ROPE) ** -0.5 * LOG2E
    taba = cat([(ca + nope) * scale_a, s1a * scale_a, s2a * scale_a, ca, s1a, s2a])
    cb = cat([np.cos(ar), np.cos(ar), np.cos(ac), np.cos(ac)] * 2)
    s1b = cat([-np.sin(ar), z(ROT), -np.sin(ac), z(ROT)] * 2)
    s2b = cat([z(ROT), np.sin(ar), z(ROT), np.sin(ac)] * 2)
    tabb = cat([cb, s1b, s2b])
    return jnp.asarray(taba, F32), jnp.asarray(tabb, F32)


def kernel(x, p, g_mix, w_in, g_qa, w_qb, g_kva, w_kvb, g_qn, g_kn, w_oa, w_ob, w_o,
           g_mlp, w_up, w_down, g_ple, w_ple_gate, w_ple, g_final):
    b, s, d = x.shape
    t = b * s
    x2 = x.reshape(t, d)
    taba, tabb = _rope_tables(s)
    seg = (jnp.arange(H_B * HD_B)[:, None] // HD_B == jnp.arange(H_B * HD_B)[None, :] // HD_B).astype(BF16)
    row = lambda g: g.reshape(1, -1)

    assert w_in.shape[0] == 1, "the block is built for the stated depth of one layer"
    n_branch = C_GA - LANES + QK_ROPE
    wgate = w_in[0][:, n_branch:].astype(BF16)
    wi = w_in[0][:, :n_branch].astype(BF16)
    o = 0
    parts = {}
    for name, w in (("qlat", Q_LORA), ("ckv", KV_LORA), ("kpe", QK_ROPE), ("qb", H_B * HD_B),
                    ("kb", KV_B * HD_B), ("vb", KV_B * HD_B)):
        parts[name] = wi[:, o:o + w]
        o += w
    assert o == n_branch and wgate.shape[1] == 2 * D_MODEL
    zc = lambda w: jnp.zeros((d, w), wi.dtype)
    win = jnp.concatenate([parts["qlat"], parts["ckv"], zc(QK_NOPE), parts["kpe"],
                           zc(LANES - QK_NOPE - QK_ROPE), parts["qb"], parts["kb"], parts["vb"]], axis=1)
    wqb = jnp.pad(w_qb[0].reshape(Q_LORA, H_A, QK_NOPE + QK_ROPE),
                  ((0, 0), (0, 0), (0, LANES - QK_NOPE - QK_ROPE))).reshape(Q_LORA, H_A * LANES).astype(BF16)
    wkv = w_kvb[0].reshape(KV_LORA, H_A, QK_NOPE + V_DIM_A)
    wk = jnp.pad(wkv[:, :, :QK_NOPE], ((0, 0), (0, 0), (0, LANES - QK_NOPE))).reshape(KV_LORA, H_A * LANES).astype(BF16)
    wvt = wkv[:, :, QK_NOPE:].reshape(KV_LORA, H_A * V_DIM_A).T.astype(BF16)

    qat, ka, vat, qbt, kbm, vbt, ga, gb = _proj_call(
        x2, row(g_mix[0]), win, wgate, row(g_qa[0]), wqb, row(g_kva[0]), wk, wvt,
        row(jnp.tile(g_qn[0], H_B)), row(jnp.tile(g_kn[0], KV_B)), seg, taba, tabb, b, s)

    group = H_B // KV_B
    per_tile = LANES // V_DIM
    heads = tuple((0, hd * LANES, hd * LANES, hd * V_DIM, hd // per_tile) for hd in range(H_A))
    heads += tuple((1, (hd // 2) * LANES, (2 * (hd // group) + hd % 2) * LANES, (hd // group) * V_DIM,
                    hd // per_tile) for hd in range(H_B))
    oa, ob = _attn_call(qat, ka.reshape(b, s, -1), vat, qbt, kbm.reshape(b, s, -1), vbt, heads)

    out = _post_call(x2, oa.reshape(t, -1), ob.reshape(t, -1), ga, gb, p[0].reshape(t, -1),
                     w_oa[0].astype(BF16), w_ob[0].astype(BF16), w_o[0].astype(BF16), row(g_mlp[0]),
                     w_up[0].astype(BF16), w_down[0].astype(BF16), row(g_ple[0]),
                     w_ple_gate[0].astype(BF16), w_ple[0].astype(BF16), row(g_final))
    return out.reshape(b, s, d)
```
